```python
import jax, jax.numpy as jnp
from jax import lax
import numpy as np

D_MODEL = 1024
BATCH = 16
SEQ = 2048
DEPTH = 1
DEC_BATCH = 128
DEC_SEQ = 1
PAST_LEN = 8192
PAGE_SIZE = 128

H_SB = 8
DH_SB = 64
W_SB = H_SB * DH_SB
SB_BLOCK = 128
SB_BIAS_INIT = -8.0
H_GLA = 4
DK_GLA = 64
DV_GLA = 128
WK_GLA = H_GLA * DK_GLA
WV_GLA = H_GLA * DV_GLA
GATE_RANK = 16
GATE_TAU = 16.0
GLA_CHUNK = 64
D_FF = 2816
EPS = 1e-6
SPLITS = (W_SB, W_SB, W_SB, WK_GLA, WK_GLA, WV_GLA, WV_GLA, GATE_RANK, D_MODEL, D_MODEL)
N_IN = W_SB * 3 + WK_GLA * 2 + WV_GLA * 2 + GATE_RANK + 2 * D_MODEL

kernel_name = "hybrid_stickbreak_gla_macaron_step"


def rms_norm(x, g):
    xf = x.astype(jnp.float32)
    y = xf * lax.rsqrt(jnp.mean(xf * xf, axis=-1, keepdims=True) + EPS)
    return (y * g.astype(jnp.float32)).astype(x.dtype)


def swiglu_ffn(x, norm_g, w_in, w_out):
    a, b = jnp.split(rms_norm(x, norm_g) @ w_in, 2, axis=-1)
    return (jax.nn.silu(a) * b) @ w_out


def stick_breaking(q, k, v, sb_bias, q_start):
    sq = q.shape[1]
    scale = 1.0 / float(np.sqrt(q.shape[-1]))
    bias = sb_bias.astype(jnp.float32)[None, :, None, None]
    outs = []
    for i0 in range(0, sq, SB_BLOCK):
        i1 = min(i0 + SB_BLOCK, sq)
        kend = q_start + i1
        z = jnp.einsum('bqhd,bkhd->bhqk', q[:, i0:i1], k[:, :kend]).astype(jnp.float32) * scale + bias
        t_pos = q_start + jnp.arange(i0, i1)
        s_pos = jnp.arange(kend)
        mask = s_pos[None, :] < t_pos[:, None]
        log_1mb = jnp.where(mask, jax.nn.log_sigmoid(-z), 0.0)
        suffix = lax.cumsum(log_1mb, axis=3, reverse=True) - log_1mb
        att = jnp.where(mask, jnp.exp(jax.nn.log_sigmoid(z) + suffix), 0.0)
        outs.append(jnp.einsum('bhqk,bkhd->bqhd', att.astype(v.dtype), v[:, :kend]))
    return jnp.concatenate(outs, axis=1)


def gla_chunked(q, k, v, log_a, s0):
    bsz, s, h, _ = q.shape
    dv = v.shape[-1]
    c = min(GLA_CHUNK, s)
    n = -(-s // c)
    pad = n * c - s

    def to_chunks(t):
        t = jnp.pad(t, ((0, 0), (0, pad), (0, 0), (0, 0)))
        return t.reshape(bsz, n, c, h, t.shape[-1]).transpose(1, 0, 3, 2, 4)

    qc, kc, vc, ac = to_chunks(q), to_chunks(k), to_chunks(v), to_chunks(log_a)
    b = jnp.cumsum(ac, axis=3)
    b_last = b[:, :, :, -1, :]
    q_dec = qc * jnp.exp(b)
    k_inv = kc * jnp.exp(-b)
    k_end = kc * jnp.exp(b_last[:, :, :, None, :] - b)
    causal = jnp.tril(jnp.ones((c, c), dtype=bool))
    att = jnp.where(causal, jnp.einsum('nbhid,nbhjd->nbhij', q_dec, k_inv), 0.0)
    o_intra = jnp.einsum('nbhij,nbhjv->nbhiv', att, vc)

    def step(state, inp):
        q_i, k_i, v_i, bl = inp
        o = jnp.einsum('bhcd,bhdv->bhcv', q_i, state)
        state = jnp.exp(bl)[..., None] * state + jnp.einsum('bhcd,bhcv->bhdv', k_i, v_i)
        return state, o

    s_fin, o_inter = lax.scan(step, s0, (q_dec, k_end, vc, b_last))
    o = (o_intra + o_inter).transpose(1, 0, 3, 2, 4).reshape(bsz, n * c, h, dv)[:, :s]
    return o, s_fin


def mixer(h, k_past, v_past, s0, q_start, w_in, b_gate, q_norm, k_norm, sb_bias, w_a2, b_a,
          gla_norm, w_o_sb, w_o_gla, w_out):
    bsz, s, _ = h.shape
    z = h @ w_in
    parts = []
    off = 0
    for width in SPLITS:
        parts.append(z[..., off:off + width])
        off += width
    q_sb, k_sb, v_sb, q_g, k_g, v_g, r_g, a_lr, g_sb, g_gla = parts

    q_sb = rms_norm(q_sb.reshape(bsz, s, H_SB, DH_SB), q_norm)
    k_sb = rms_norm(k_sb.reshape(bsz, s, H_SB, DH_SB), k_norm)
    v_sb = v_sb.reshape(bsz, s, H_SB, DH_SB)
    if k_past is None:
        k_all, v_all = k_sb, v_sb
    else:
        k_all = jnp.concatenate([k_past.astype(k_sb.dtype), k_sb], axis=1)
        v_all = jnp.concatenate([v_past.astype(v_sb.dtype), v_sb], axis=1)
    o_sb = stick_breaking(q_sb, k_all, v_all, sb_bias, q_start).reshape(bsz, s, W_SB)

    f32 = jnp.float32
    qg = q_g.reshape(bsz, s, H_GLA, DK_GLA).astype(f32) * (DK_GLA ** -0.5)
    kg = k_g.reshape(bsz, s, H_GLA, DK_GLA).astype(f32)
    vg = v_g.reshape(bsz, s, H_GLA, DV_GLA).astype(f32)
    log_a = jax.nn.log_sigmoid((a_lr @ w_a2 + b_a).astype(f32)) / GATE_TAU
    log_a = log_a.reshape(bsz, s, H_GLA, DK_GLA)
    o_g, s_new = gla_chunked(qg, kg, vg, log_a, s0.astype(f32))
    o_g = rms_norm(o_g, gla_norm).reshape(bsz, s, WV_GLA).astype(h.dtype) * jax.nn.silu(r_g)

    gates = jax.nn.sigmoid(jnp.concatenate([g_sb, g_gla], axis=-1) + b_gate)
    gate_sb, gate_gla = jnp.split(gates, 2, axis=-1)
    merged = gate_sb * (o_sb @ w_o_sb) + gate_gla * (o_g @ w_o_gla)
    return merged @ w_out, k_sb, v_sb, s_new


def trunk_layer(x, k_past, v_past, s0, q_start, ffn1_norm, ffn1_w_in, ffn1_w_out, mix_norm,
                w_in, b_gate, q_norm, k_norm, sb_bias, w_a2, b_a, gla_norm, w_o_sb, w_o_gla,
                w_out, ffn2_norm, ffn2_w_in, ffn2_w_out):
    x = x + 0.5 * swiglu_ffn(x, ffn1_norm, ffn1_w_in, ffn1_w_out)
    m, k_new, v_new, s_new = mixer(rms_norm(x, mix_norm), k_past, v_past, s0, q_start, w_in,
                                   b_gate, q_norm, k_norm, sb_bias, w_a2, b_a, gla_norm,
                                   w_o_sb, w_o_gla, w_out)
    x = x + m
    x = x + 0.5 * swiglu_ffn(x, ffn2_norm, ffn2_w_in, ffn2_w_out)
    return x, k_new, v_new, s_new


def setup_inputs(seed: int = 0) -> dict:
    key = jax.random.key(seed)
    ks = jax.random.split(key, 32)
    f32 = jnp.float32

    def nrm(k, shape, scale):
        return jax.random.normal(k, shape, f32) * scale

    def gain(k, shape):
        return 1.0 + 0.02 * jax.random.normal(k, shape, f32)

    n_pages = PAST_LEN // PAGE_SIZE
    n_used = DEC_BATCH * n_pages
    n_pool = n_used + max(1, n_used // 4)
    page_table = jax.random.permutation(ks[0], n_pool)[:n_used].reshape(DEC_BATCH, n_pages).astype(jnp.int32)
    L = DEPTH
    return {
        "x_prompt": nrm(ks[1], (BATCH, SEQ, D_MODEL), 1.0),
        "x_sample": nrm(ks[2], (DEC_BATCH, DEC_SEQ, D_MODEL), 1.0),
        "cache_k": nrm(ks[3], (L, n_pool, PAGE_SIZE, H_SB, DH_SB), 1.0),
        "cache_v": nrm(ks[4], (L, n_pool, PAGE_SIZE, H_SB, DH_SB), 1.0),
        "state_gla": nrm(ks[5], (L, DEC_BATCH, H_GLA, DK_GLA, DV_GLA), 1.0),
        "page_table": page_table,
        "ffn1_norm": gain(ks[6], (L, D_MODEL)),
        "ffn1_w_in": nrm(ks[7], (L, D_MODEL, 2 * D_FF), D_MODEL ** -0.5),
        "ffn1_w_out": nrm(ks[8], (L, D_FF, D_MODEL), D_FF ** -0.5),
        "mix_norm": gain(ks[9], (L, D_MODEL)),
        "w_in": nrm(ks[10], (L, D_MODEL, N_IN), D_MODEL ** -0.5),
        "b_gate": nrm(ks[11], (L, 2 * D_MODEL), 0.01),
        "q_norm": gain(ks[12], (L, DH_SB)),
        "k_norm": gain(ks[13], (L, DH_SB)),
        "sb_bias": SB_BIAS_INIT + nrm(ks[23], (L, H_SB), 0.5),
        "w_a2": nrm(ks[14], (L, GATE_RANK, WK_GLA), GATE_RANK ** -0.5),
        "b_a": nrm(ks[15], (L, WK_GLA), 0.01),
        "gla_norm": gain(ks[16], (L, DV_GLA)),
        "w_o_sb": nrm(ks[17], (L, W_SB, D_MODEL), W_SB ** -0.5),
        "w_o_gla": nrm(ks[18], (L, WV_GLA, D_MODEL), WV_GLA ** -0.5),
        "w_out": nrm(ks[19], (L, D_MODEL, D_MODEL), D_MODEL ** -0.5),
        "ffn2_norm": gain(ks[20], (L, D_MODEL)),
        "ffn2_w_in": nrm(ks[21], (L, D_MODEL, 2 * D_FF), D_MODEL ** -0.5),
        "ffn2_w_out": nrm(ks[22], (L, D_FF, D_MODEL), D_FF ** -0.5),
    }


def reference(x_prompt, x_sample, cache_k, cache_v, state_gla, page_table, ffn1_norm,
              ffn1_w_in, ffn1_w_out, mix_norm, w_in, b_gate, q_norm, k_norm, sb_bias, w_a2,
              b_a, gla_norm, w_o_sb, w_o_gla, w_out, ffn2_norm, ffn2_w_in, ffn2_w_out):
    bp = x_prompt.shape[0]
    bd = x_sample.shape[0]
    past_len = page_table.shape[1] * cache_k.shape[2]
    yp, ys = x_prompt, x_sample
    kp_l, vp_l, sp_l, ks_l, vs_l, ss_l = [], [], [], [], [], []
    for l in range(DEPTH):
        w = (ffn1_norm[l], ffn1_w_in[l], ffn1_w_out[l], mix_norm[l], w_in[l], b_gate[l],
             q_norm[l], k_norm[l], sb_bias[l], w_a2[l], b_a[l], gla_norm[l], w_o_sb[l],
             w_o_gla[l], w_out[l], ffn2_norm[l], ffn2_w_in[l], ffn2_w_out[l])
        s0 = jnp.zeros((bp, H_GLA, DK_GLA, DV_GLA), jnp.float32)
        yp, kp, vp, sp = trunk_layer(yp, None, None, s0, 0, *w)
        k_past = cache_k[l][page_table].reshape(bd, past_len, H_SB, DH_SB)
        v_past = cache_v[l][page_table].reshape(bd, past_len, H_SB, DH_SB)
        ys, kd, vd, sd = trunk_layer(ys, k_past, v_past, state_gla[l], past_len, *w)
        kp_l.append(kp); vp_l.append(vp); sp_l.append(sp)
        ks_l.append(kd); vs_l.append(vd); ss_l.append(sd)
    k_prompt = jnp.stack(kp_l)
    v_prompt = jnp.stack(vp_l)
    state_prompt = jnp.stack(sp_l)
    k_sample = jnp.stack(ks_l)
    v_sample = jnp.stack(vs_l)
    state_sample = jnp.stack(ss_l)
    return (yp, ys, k_prompt, v_prompt, state_prompt, k_sample, v_sample, state_sample)
```

```python
import functools

import jax
import jax.numpy as jnp
from jax import lax
from jax.experimental import pallas as pl
from jax.experimental.pallas import tpu as pltpu

F32 = jnp.float32
BF16 = jnp.bfloat16

EPS = 1e-6
H_SB = 8
DH_SB = 64
W_SB = H_SB * DH_SB
H_GLA = 4
DK_GLA = 64
DV_GLA = 128
WK_GLA = H_GLA * DK_GLA
WV_GLA = H_GLA * DV_GLA
GATE_RANK = 16
GATE_TAU = 16.0
GLA_CHUNK = 64

LANES = 128
VMEM_LIMIT_BYTES = 56 * 1024 * 1024
SB_TILE = 256
DEC_PAGES_PER_STEP = 8

_NT = (((1,), (1,)), ((), ()))
_TN = (((0,), (0,)), ((), ()))


def _dot(a, b):
    return jnp.dot(a, b, preferred_element_type=F32)


def _dot_nt(a, b):
    return lax.dot_general(a, b, _NT, preferred_element_type=F32)


def _dot_tn(a, b):
    return lax.dot_general(a, b, _TN, preferred_element_type=F32)


def _rms(x, g):
    return x * lax.rsqrt(jnp.mean(x * x, axis=-1, keepdims=True) + EPS) * g


def _neg_softplus(z):
    return -(jnp.maximum(z, 0.0) + jnp.log1p(jnp.exp(-jnp.abs(z))))


def _split_bf16(x):
    hi = x.astype(BF16)
    lo = (x - hi.astype(F32)).astype(BF16)
    return hi, lo


def _params(*sem):
    return pltpu.CompilerParams(dimension_semantics=sem, vmem_limit_bytes=VMEM_LIMIT_BYTES)


def _full(shape):
    nd = len(shape)
    return pl.BlockSpec(shape, lambda *_: (0,) * nd)


def _ffn_body(x_ref, g_ref, wa_ref, wb_ref, wo_ref, o_ref, *, chunks):
    x = x_ref[...]
    xn = _rms(x, g_ref[...]).astype(BF16)
    acc = None
    for c0, cw in chunks:
        a = _dot(xn, wa_ref[:, c0:c0 + cw])
        b = _dot(xn, wb_ref[:, c0:c0 + cw])
        h = (a * jax.nn.sigmoid(a) * b).astype(BF16)
        y = _dot(h, wo_ref[c0:c0 + cw, :])
        acc = y if acc is None else acc + y
    o_ref[...] = x + 0.5 * acc


def _ffn(x, g, wa, wb, wo, tm):
    t, d = x.shape
    f = wa.shape[1]
    half = f // 2
    assert half % LANES == 0
    chunks = ((0, half), (half, f - half))
    row = pl.BlockSpec((tm, d), lambda i: (i, 0))
    return pl.pallas_call(
        functools.partial(_ffn_body, chunks=chunks),
        grid=(t // tm,),
        in_specs=[row, _full((1, d)), _full((d, f)), _full((d, f)), _full((f, d))],
        out_specs=row,
        out_shape=jax.ShapeDtypeStruct((t, d), F32),
        compiler_params=_params("parallel"),
        name="ffn",
    )(x, g, wa, wb, wo)


def _inproj_body(x_ref, g_ref, wsb_ref, wgl_ref, walr_ref, wa2_ref, ba_ref, wgt_ref, bgt_ref,
                 qn_ref, kn_ref, blk_ref,
                 q_ref, k_ref, v_ref, kb_ref, vb_ref, qg_ref, kg_ref, la_ref, vg_ref, rg_ref,
                 gt_ref):
    h = _rms(x_ref[...], g_ref[...]).astype(BF16)

    def head_norm(t, gain):
        ss = _dot((t * t).astype(BF16), blk_ref[...])
        return t * lax.rsqrt(ss * (1.0 / DH_SB) + EPS) * gain

    zsb = _dot(h, wsb_ref[...])
    q = head_norm(zsb[:, :W_SB], qn_ref[...])
    k = head_norm(zsb[:, W_SB:2 * W_SB], kn_ref[...])
    v = zsb[:, 2 * W_SB:]
    q_ref[...] = (q * (DH_SB ** -0.5)).astype(BF16)
    k_ref[...] = k
    v_ref[...] = v
    kb_ref[...] = k.astype(BF16)
    vb_ref[...] = v.astype(BF16)

    zgl = _dot(h, wgl_ref[...])
    qg_ref[...] = zgl[:, :WK_GLA] * (DK_GLA ** -0.5)
    kg_ref[...] = zgl[:, WK_GLA:2 * WK_GLA]
    vg_ref[...] = zgl[:, 2 * WK_GLA:2 * WK_GLA + WV_GLA]
    rg_ref[...] = zgl[:, 2 * WK_GLA + WV_GLA:]

    alr = _dot(h, walr_ref[...])
    lin = _dot(alr.astype(BF16), wa2_ref[...]) + ba_ref[...]
    la_ref[...] = _neg_softplus(-lin) * (1.0 / GATE_TAU)

    gt_ref[...] = jax.nn.sigmoid(_dot(h, wgt_ref[...]) + bgt_ref[...]).astype(BF16)


def _inproj(x, w, tm):
    t, d = x.shape
    row = lambda n: pl.BlockSpec((tm, n), lambda i: (i, 0))
    ins = [x, w["mix_norm"], w["w_sb"], w["w_gl"], w["w_alr"], w["w_a2"], w["b_a"], w["w_gt"],
           w["b_gt"], w["q_norm"], w["k_norm"], w["blk"]]
    in_specs = [row(d)] + [_full(a.shape) for a in ins[1:]]
    outs = [(W_SB, BF16), (W_SB, F32), (W_SB, F32), (W_SB, BF16), (W_SB, BF16),
            (WK_GLA, F32), (WK_GLA, F32), (WK_GLA, F32), (WV_GLA, F32), (WV_GLA, F32),
            (2 * d, BF16)]
    return pl.pallas_call(
        _inproj_body,
        grid=(t // tm,),
        in_specs=in_specs,
        out_specs=[row(n) for n, _ in outs],
        out_shape=[jax.ShapeDtypeStruct((t, n), dt) for n, dt in outs],
        compiler_params=_params("parallel"),
        name="inproj",
    )(*ins)


def _sb_prompt_body(bias_ref, q_ref, k_ref, v_ref, tri_ref, o_ref):
    pair = pl.program_id(1)
    qi = pl.program_id(2)
    tq = q_ref.shape[0]
    lane = lax.broadcasted_iota(jnp.int32, (1, LANES), 1)
    row = lax.broadcasted_iota(jnp.int32, (tq, tq), 0)
    col = lax.broadcasted_iota(jnp.int32, (tq, tq), 1)
    causal = col < row
    q2 = q_ref[...]
    tri = tri_ref[...]
    outs = []
    for hh in range(2):
        in_head = (lane >= hh * DH_SB) & (lane < (hh + 1) * DH_SB)
        qa = jnp.where(in_head, q2, jnp.zeros_like(q2))
        bias = bias_ref[2 * pair + hh]

        def block(kb, carry, acc, masked):
            start = pl.multiple_of(kb * tq, tq)
            kblk = k_ref[pl.ds(start, tq), :]
            vblk = v_ref[pl.ds(start, tq), :]
            z = _dot_nt(qa, kblk) + bias
            l1m = _neg_softplus(z)
            if masked:
                l1m = jnp.where(causal, l1m, 0.0)
            suffix = _dot(l1m.astype(BF16), tri) + carry
            att = jnp.exp(z + l1m + suffix)
            if masked:
                att = jnp.where(causal, att, 0.0)
            acc = acc + _dot(att.astype(BF16), vblk)
            carry = carry + jnp.sum(l1m, axis=1, keepdims=True)
            return carry, acc

        carry, acc = block(qi, jnp.zeros((tq, 1), F32), jnp.zeros((tq, LANES), F32), True)
        carry, acc = lax.fori_loop(
            0, qi, lambda i, c: block(qi - 1 - i, c[0], c[1], False), (carry, acc))
        outs.append(acc)
    o_ref[...] = jnp.where(lane < DH_SB, outs[0], outs[1]).astype(o_ref.dtype)


def _sb_prompt(q, k, v, bias, tri):
    b, s, w = q.shape
    tq = SB_TILE
    qspec = pl.BlockSpec((None, tq, LANES), lambda bi, p, i, *_: (bi, i, p))
    kvspec = pl.BlockSpec((None, s, LANES), lambda bi, p, i, *_: (bi, 0, p))
    return pl.pallas_call(
        _sb_prompt_body,
        grid_spec=pltpu.PrefetchScalarGridSpec(
            num_scalar_prefetch=1,
            grid=(b, w // LANES, s // tq),
            in_specs=[qspec, kvspec, kvspec, pl.BlockSpec((tq, tq), lambda *_: (0, 0))],
            out_specs=qspec,
        ),
        out_shape=jax.ShapeDtypeStruct((b, s, w), BF16),
        compiler_params=_params("parallel", "parallel", "arbitrary"),
        name="sb_prompt",
    )(bias, q, k, v, tri)


def _gla_prompt_body(q_ref, k_ref, la_ref, v_ref, r_ref, gn_ref, tril_ref, o_ref, st_ref, stt_ref):
    c = pl.program_id(0)
    nb = q_ref.shape[0]
    cs = q_ref.shape[1]

    @pl.when(c == 0)
    def _():
        stt_ref[...] = jnp.zeros_like(stt_ref)

    lane = lax.broadcasted_iota(jnp.int32, (1, LANES), 1)
    head_lo = lane < DK_GLA
    ti = lax.broadcasted_iota(jnp.int32, (cs, cs), 0)
    tj = lax.broadcasted_iota(jnp.int32, (cs, cs), 1)
    causal = tj <= ti
    tril = tril_ref[...]
    gn = gn_ref[...]

    def per_batch(b, _):
        la = la_ref[b]
        la_hi, la_lo = _split_bf16(la)
        bc = _dot(tril, la_hi) + _dot(tril, la_lo)
        b_last = bc[cs - 1:cs, :]
        q_dec = q_ref[b] * jnp.exp(bc)
        k_inv = (k_ref[b] * jnp.exp(-bc)).astype(BF16)
        k_end = (k_ref[b] * jnp.exp(b_last - bc)).astype(BF16)
        decay = jnp.exp(b_last)
        vb = v_ref[b].astype(BF16)
        rb = r_ref[b]
        for p in range(H_GLA // 2):
            ps = slice(p * LANES, (p + 1) * LANES)
            qd2 = q_dec[:, ps]
            ki2 = k_inv[:, ps]
            ke2 = k_end[:, ps]
            stt = stt_ref[b, p]
            stt_b = stt.astype(BF16)
            kvs = []
            for hh in range(2):
                h = 2 * p + hh
                hs = slice(h * DV_GLA, (h + 1) * DV_GLA)
                in_head = head_lo if hh == 0 else jnp.logical_not(head_lo)
                qa = jnp.where(in_head, qd2, 0.0).astype(BF16)
                att = jnp.where(causal, _dot_nt(qa, ki2), 0.0)
                vh = vb[:, hs]
                o = _dot(att.astype(BF16), vh) + _dot_nt(qa, stt_b)
                o = _rms(o, gn)
                r = rb[:, hs]
                o_ref[b, :, hs] = (o * (r * jax.nn.sigmoid(r))).astype(o_ref.dtype)
                kvs.append(_dot_tn(vh, ke2))
            stt_ref[b, p] = decay[:, ps] * stt + jnp.where(head_lo, kvs[0], kvs[1])
        return 0

    lax.fori_loop(0, nb, per_batch, 0)

    @pl.when(c == pl.num_programs(0) - 1)
    def _():
        def finish(b, _):
            for p in range(H_GLA // 2):
                st_ref[b, p] = stt_ref[b, p].T
            return 0
        lax.fori_loop(0, nb, finish, 0)


def _gla_prompt(qg, kg, la, vg, rg, gn, tril):
    b, s, _ = qg.shape
    cs = min(GLA_CHUNK, s)
    assert s % cs == 0
    chunk = lambda n: pl.BlockSpec((b, cs, n), lambda c: (0, c, 0))
    o, st = pl.pallas_call(
        _gla_prompt_body,
        grid=(s // cs,),
        in_specs=[chunk(WK_GLA), chunk(WK_GLA), chunk(WK_GLA), chunk(WV_GLA), chunk(WV_GLA),
                  _full(gn.shape), _full(tril.shape)],
        out_specs=[chunk(WV_GLA), _full((b, H_GLA // 2, 2 * DK_GLA, DV_GLA))],
        out_shape=[jax.ShapeDtypeStruct((b, s, WV_GLA), BF16),
                   jax.ShapeDtypeStruct((b, H_GLA // 2, 2 * DK_GLA, DV_GLA), F32)],
        scratch_shapes=[pltpu.VMEM((b, H_GLA // 2, DV_GLA, 2 * DK_GLA), F32)],
        compiler_params=_params("arbitrary"),
        name="gla_prompt",
    )(qg, kg, la, vg, rg, gn, tril)
    return o, st.reshape(b, H_GLA, DK_GLA, DV_GLA)


def _merge_body(x_ref, osb_ref, og_ref, gt_ref, wosb_ref, wogl_ref, wout_ref, o_ref):
    d = x_ref.shape[1]
    g = gt_ref[...].astype(F32)
    m = g[:, :d] * _dot(osb_ref[...], wosb_ref[...]) + g[:, d:] * _dot(og_ref[...], wogl_ref[...])
    o_ref[...] = x_ref[...] + _dot(m.astype(BF16), wout_ref[...])


def _merge(x, osb, og, gt, w, tm):
    t, d = x.shape
    row = lambda n: pl.BlockSpec((tm, n), lambda i: (i, 0))
    return pl.pallas_call(
        _merge_body,
        grid=(t // tm,),
        in_specs=[row(d), row(W_SB), row(WV_GLA), row(2 * d),
                  _full(w["w_o_sb"].shape), _full(w["w_o_gla"].shape), _full(w["w_out"].shape)],
        out_specs=row(d),
        out_shape=jax.ShapeDtypeStruct((t, d), F32),
        compiler_params=_params("parallel"),
        name="merge",
    )(x, osb, og, gt, w["w_o_sb"], w["w_o_gla"], w["w_out"])


def _sb_decode_body(pt_ref, q_ref, kn_ref, vn_ref, bias_ref, tri_ref, *rest, pages, past_len):
    k_refs = rest[:pages]
    v_refs = rest[pages:2 * pages]
    o_ref, acc_ref, carry_ref = rest[2 * pages:]
    j = pl.program_id(1)
    w = q_ref.shape[1]
    rowid = lax.broadcasted_iota(jnp.int32, (H_SB, w), 0)
    lanehead = lax.broadcasted_iota(jnp.int32, (H_SB, w), 1) // DH_SB
    own = rowid == lanehead
    qbd = jnp.where(own, jnp.broadcast_to(q_ref[...].astype(F32), (H_SB, w)), 0.0).astype(BF16)
    bias = bias_ref[...]
    tri = tri_ref[...]

    @pl.when(j == 0)
    def _():
        s_pos = past_len + lax.broadcasted_iota(jnp.int32, (H_SB, 1), 1)
        visible = s_pos < past_len
        z = jnp.sum(qbd.astype(F32) * kn_ref[...], axis=1, keepdims=True) + bias[:, :1]
        l1m = jnp.where(visible, _neg_softplus(z), 0.0)
        att = jnp.where(visible, jnp.exp(z + l1m), 0.0)
        acc_ref[...] = att * jnp.broadcast_to(vn_ref[...], (H_SB, w))
        carry_ref[...] = jnp.broadcast_to(l1m, carry_ref.shape)

    carry = carry_ref[:, :1]
    acc = acc_ref[...]
    for p in range(pages):
        kp = k_refs[p][...].astype(BF16)
        vp = v_refs[p][...].astype(BF16)
        z = _dot_nt(qbd, kp) + bias
        l1m = _neg_softplus(z)
        hi, lo = _split_bf16(l1m)
        suffix = _dot(hi, tri) + _dot(lo, tri) + carry
        att = jnp.exp(z + l1m + suffix)
        acc = acc + _dot(att.astype(BF16), vp)
        carry = carry + jnp.sum(l1m, axis=1, keepdims=True)
    acc_ref[...] = acc
    carry_ref[...] = jnp.broadcast_to(carry, carry_ref.shape)

    @pl.when(j == pl.num_programs(1) - 1)
    def _():
        o_ref[...] = jnp.sum(jnp.where(own, acc, 0.0), axis=0, keepdims=True).astype(o_ref.dtype)


def _sb_decode(q, k_new, v_new, cache_k, cache_v, pool_offset, page_table, bias8, tri):
    nb, n_pages = page_table.shape
    _, page, w = cache_k.shape
    pages = DEC_PAGES_PER_STEP
    assert n_pages % pages == 0 and page == LANES
    steps = n_pages // pages
    past_len = n_pages * page

    def page_spec(p):
        return pl.BlockSpec(
            (None, page, w),
            lambda b, j, pt: (pool_offset + pt[b, n_pages - 1 - (j * pages + p)], 0, 0))

    vec = pl.BlockSpec((None, 1, w), lambda b, j, pt: (b, 0, 0))
    const = lambda shape: pl.BlockSpec(shape, lambda b, j, pt: (0,) * len(shape))
    return pl.pallas_call(
        functools.partial(_sb_decode_body, pages=pages, past_len=past_len),
        grid_spec=pltpu.PrefetchScalarGridSpec(
            num_scalar_prefetch=1,
            grid=(nb, steps),
            in_specs=[vec, vec, vec, const(bias8.shape), const(tri.shape)]
            + [page_spec(p) for p in range(pages)] + [page_spec(p) for p in range(pages)],
            out_specs=vec,
            scratch_shapes=[pltpu.VMEM((H_SB, w), F32), pltpu.VMEM((H_SB, LANES), F32)],
        ),
        out_shape=jax.ShapeDtypeStruct((nb, 1, w), BF16),
        compiler_params=_params("parallel", "arbitrary"),
        name="sb_decode",
    )(page_table, q, k_new, v_new, bias8, tri, *([cache_k] * pages), *([cache_v] * pages))


def _gla_step_body(q_ref, k_ref, la_ref, v_ref, r_ref, gn_ref, s0_ref, o_ref, s1_ref):
    nb = q_ref.shape[0]
    rowid = lax.broadcasted_iota(jnp.int32, (8, WK_GLA), 0)
    own = rowid == lax.broadcasted_iota(jnp.int32, (8, WK_GLA), 1) // DK_GLA
    vrow = lax.broadcasted_iota(jnp.int32, (8, DV_GLA), 0)
    ones = jnp.ones((8, DV_GLA), BF16)
    gn = gn_ref[...]

    def rows_of_heads(x):
        out = jnp.broadcast_to(x[:, :DV_GLA], (8, DV_GLA))
        for h in range(1, H_GLA):
            out = jnp.where(vrow == h, jnp.broadcast_to(x[:, h * DV_GLA:(h + 1) * DV_GLA],
                                                        (8, DV_GLA)), out)
        return out

    for i in range(nb):
        q = q_ref[i:i + 1, :]
        k = k_ref[i:i + 1, :]
        decay = jnp.exp(la_ref[i:i + 1, :])
        s0 = s0_ref[i]
        v8 = rows_of_heads(v_ref[i:i + 1, :])
        r8 = rows_of_heads(r_ref[i:i + 1, :])
        spread = lambda x: jnp.where(own, jnp.broadcast_to(x, (8, WK_GLA)), 0.0)
        qk = jnp.sum(spread(q * k), axis=1, keepdims=True)
        o = qk * v8 + _dot(spread(q * decay).astype(BF16), s0.astype(BF16))
        o = _rms(o, gn) * (r8 * jax.nn.sigmoid(r8))
        for h in range(H_GLA):
            o_ref[i:i + 1, h * DV_GLA:(h + 1) * DV_GLA] = o[h:h + 1, :].astype(o_ref.dtype)
        d_hi = decay.astype(BF16).astype(F32)
        dd = jnp.where(rowid == 0, jnp.broadcast_to(d_hi, (8, WK_GLA)),
                       jnp.where(rowid == 1, jnp.broadcast_to(decay - d_hi, (8, WK_GLA)), 0.0))
        decay_col = _dot_tn(dd.astype(BF16), ones)
        s1_ref[i] = decay_col * s0 + _dot_tn(spread(k).astype(BF16), v8.astype(BF16))


def _gla_step(qg, kg, la, vg, rg, gn, s0, layer, nb_blk=8):
    nb = qg.shape[0]
    blk0 = layer * (nb // nb_blk)
    row = lambda n: pl.BlockSpec((nb_blk, n), lambda i: (i, 0))
    st = pl.BlockSpec((nb_blk, WK_GLA, DV_GLA), lambda i: (i, 0, 0))
    st_in = pl.BlockSpec((nb_blk, WK_GLA, DV_GLA), lambda i: (blk0 + i, 0, 0))
    o, s1 = pl.pallas_call(
        _gla_step_body,
        grid=(nb // nb_blk,),
        in_specs=[row(WK_GLA), row(WK_GLA), row(WK_GLA), row(WV_GLA), row(WV_GLA),
                  _full(gn.shape), st_in],
        out_specs=[row(WV_GLA), st],
        out_shape=[jax.ShapeDtypeStruct((nb, WV_GLA), BF16),
                   jax.ShapeDtypeStruct((nb, WK_GLA, DV_GLA), F32)],
        compiler_params=_params("parallel"),
        name="gla_step",
    )(qg, kg, la, vg, rg, gn, s0)
    return o, s1.reshape(nb, H_GLA, DK_GLA, DV_GLA)


def _prep_layer(l, ffn1_norm, ffn1_w_in, ffn1_w_out, mix_norm, w_in, b_gate, q_norm, k_norm,
                sb_bias, w_a2, b_a, gla_norm, w_o_sb, w_o_gla, w_out, ffn2_norm, ffn2_w_in,
                ffn2_w_out):
    d = w_in.shape[1]
    d_ff = ffn1_w_out.shape[1]
    bf = lambda a: a.astype(BF16)
    o_gl = 3 * W_SB
    o_alr = o_gl + 2 * WK_GLA + 2 * WV_GLA
    o_gt = o_alr + GATE_RANK
    win = w_in[l]
    head = jnp.arange(W_SB) // DH_SB
    return {
        "ffn1": (ffn1_norm[l][None], bf(ffn1_w_in[l][:, :d_ff]), bf(ffn1_w_in[l][:, d_ff:]),
                 bf(ffn1_w_out[l])),
        "ffn2": (ffn2_norm[l][None], bf(ffn2_w_in[l][:, :d_ff]), bf(ffn2_w_in[l][:, d_ff:]),
                 bf(ffn2_w_out[l])),
        "mix_norm": mix_norm[l][None],
        "w_sb": bf(win[:, :o_gl]),
        "w_gl": bf(win[:, o_gl:o_alr]),
        "w_alr": bf(jnp.pad(win[:, o_alr:o_gt], ((0, 0), (0, LANES - GATE_RANK)))),
        "w_a2": bf(jnp.pad(w_a2[l], ((0, LANES - GATE_RANK), (0, 0)))),
        "b_a": b_a[l][None],
        "w_gt": bf(win[:, o_gt:]),
        "b_gt": b_gate[l][None],
        "q_norm": jnp.tile(q_norm[l], H_SB)[None],
        "k_norm": jnp.tile(k_norm[l], H_SB)[None],
        "blk": (head[:, None] == head[None, :]).astype(BF16),
        "sb_bias": sb_bias[l],
        "gla_norm": gla_norm[l][None],
        "w_o_sb": bf(w_o_sb[l]),
        "w_o_gla": bf(w_o_gla[l]),
        "w_out": bf(w_out[l]),
    }


def _strict_upper(n):
    i = jnp.arange(n)
    return (i[:, None] > i[None, :]).astype(BF16)


def _prompt_layer(x, w):
    b, s, d = x.shape
    tm = 512
    x2 = x.reshape(b * s, d)
    x2 = _ffn(x2, *w["ffn1"], tm)
    q, k, v, kb, vb, qg, kg, la, vg, rg, gt = _inproj(x2, w, tm)
    sh = lambda a: a.reshape(b, s, a.shape[-1])
    osb = _sb_prompt(sh(q), sh(kb), sh(vb), w["sb_bias"], _strict_upper(SB_TILE))
    cs = min(GLA_CHUNK, s)
    i = jnp.arange(cs)
    tril = (i[None, :] <= i[:, None]).astype(BF16)
    og, st = _gla_prompt(sh(qg), sh(kg), sh(la), sh(vg), sh(rg), w["gla_norm"], tril)
    x2 = _merge(x2, osb.reshape(b * s, W_SB), og.reshape(b * s, WV_GLA), gt, w, tm)
    x2 = _ffn(x2, *w["ffn2"], tm)
    return (x2.reshape(b, s, d), k.reshape(b, s, H_SB, DH_SB), v.reshape(b, s, H_SB, DH_SB), st)


def _sample_layer(x, cache_k, cache_v, state, page_table, layer, w):
    b, s, d = x.shape
    assert s == 1
    tm = b
    x2 = x.reshape(b, d)
    x2 = _ffn(x2, *w["ffn1"], tm)
    q, k, v, _, _, qg, kg, la, vg, rg, gt = _inproj(x2, w, tm)
    depth, n_pool, page = cache_k.shape[:3]
    bias8 = jnp.broadcast_to(w["sb_bias"][:, None], (H_SB, LANES))
    osb = _sb_decode(q.reshape(b, 1, W_SB), k.reshape(b, 1, W_SB), v.reshape(b, 1, W_SB),
                     cache_k.reshape(depth * n_pool, page, W_SB),
                     cache_v.reshape(depth * n_pool, page, W_SB),
                     layer * n_pool, page_table, bias8, _strict_upper(page))
    og, st = _gla_step(qg, kg, la, vg, rg, w["gla_norm"],
                       state.reshape(depth * b, WK_GLA, DV_GLA), layer)
    x2 = _merge(x2, osb.reshape(b, W_SB), og, gt, w, tm)
    x2 = _ffn(x2, *w["ffn2"], tm)
    return (x2.reshape(b, s, d), k.reshape(b, s, H_SB, DH_SB), v.reshape(b, s, H_SB, DH_SB), st)


def kernel(x_prompt, x_sample, cache_k, cache_v, state_gla, page_table, ffn1_norm, ffn1_w_in,
           ffn1_w_out, mix_norm, w_in, b_gate, q_norm, k_norm, sb_bias, w_a2, b_a, gla_norm,
           w_o_sb, w_o_gla, w_out, ffn2_norm, ffn2_w_in, ffn2_w_out):
    depth = w_in.shape[0]
    yp, ys = x_prompt, x_sample
    outs = [[] for _ in range(6)]
    for l in range(depth):
        w = _prep_layer(l, ffn1_norm, ffn1_w_in, ffn1_w_out, mix_norm, w_in, b_gate, q_norm,
                        k_norm, sb_bias, w_a2, b_a, gla_norm, w_o_sb, w_o_gla, w_out, ffn2_norm,
                        ffn2_w_in, ffn2_w_out)
        yp, kp, vp, sp = _prompt_layer(yp, w)
        ys, kd, vd, sd = _sample_layer(ys, cache_k, cache_v, state_gla, page_table, l, w)
        for acc, val in zip(outs, (kp, vp, sp, kd, vd, sd)):
            acc.append(val)
    return (yp, ys) + tuple(jnp.stack(o) for o in outs)
```

```python
import functools

import jax
import jax.numpy as jnp
from jax import lax
from jax.experimental import pallas as pl
from jax.experimental.pallas import tpu as pltpu

F32 = jnp.float32
BF16 = jnp.bfloat16

EPS = 1e-6
H_SB = 8
DH_SB = 64
W_SB = H_SB * DH_SB
H_GLA = 4
DK_GLA = 64
DV_GLA = 128
WK_GLA = H_GLA * DK_GLA
WV_GLA = H_GLA * DV_GLA
GATE_RANK = 16
GATE_TAU = 16.0
GLA_CHUNK = 64

LANES = 128
VMEM_LIMIT_BYTES = 56 * 1024 * 1024
SB_TILE = 256
DEC_PAGES_PER_STEP = 8

_NT = (((1,), (1,)), ((), ()))
_TN = (((0,), (0,)), ((), ()))


def _dot(a, b):
    return jnp.dot(a, b, preferred_element_type=F32)


def _dot_nt(a, b):
    return lax.dot_general(a, b, _NT, preferred_element_type=F32)


def _dot_tn(a, b):
    return lax.dot_general(a, b, _TN, preferred_element_type=F32)


def _rms(x, g):
    return x * lax.rsqrt(jnp.mean(x * x, axis=-1, keepdims=True) + EPS) * g


def _neg_softplus(z):
    return -(jnp.maximum(z, 0.0) + jnp.log1p(jnp.exp(-jnp.abs(z))))


def _split_bf16(x):
    hi = x.astype(BF16)
    lo = (x - hi.astype(F32)).astype(BF16)
    return hi, lo


def _params(*sem):
    return pltpu.CompilerParams(dimension_semantics=sem, vmem_limit_bytes=VMEM_LIMIT_BYTES)


def _full(shape):
    nd = len(shape)
    return pl.BlockSpec(shape, lambda *_: (0,) * nd)


def _ffn_body(x_ref, g_ref, wa_ref, wb_ref, wo_ref, o_ref, *, chunks):
    x = x_ref[...]
    xn = _rms(x, g_ref[...]).astype(BF16)
    acc = None
    for c0, cw in chunks:
        a = _dot(xn, wa_ref[:, c0:c0 + cw])
        b = _dot(xn, wb_ref[:, c0:c0 + cw])
        h = (a * jax.nn.sigmoid(a) * b).astype(BF16)
        y = _dot(h, wo_ref[c0:c0 + cw, :])
        acc = y if acc is None else acc + y
    o_ref[...] = x + 0.5 * acc


def _ffn(x, g, wa, wb, wo, tm):
    t, d = x.shape
    f = wa.shape[1]
    half = f // 2
    assert half % LANES == 0
    chunks = ((0, half), (half, f - half))
    row = pl.BlockSpec((tm, d), lambda i: (i, 0))
    return pl.pallas_call(
        functools.partial(_ffn_body, chunks=chunks),
        grid=(t // tm,),
        in_specs=[row, _full((1, d)), _full((d, f)), _full((d, f)), _full((f, d))],
        out_specs=row,
        out_shape=jax.ShapeDtypeStruct((t, d), F32),
        compiler_params=_params("parallel"),
        name="ffn",
    )(x, g, wa, wb, wo)


def _inproj_body(x_ref, g_ref, wsb_ref, wgl_ref, walr_ref, wa2_ref, ba_ref, wgt_ref, bgt_ref,
                 qn_ref, kn_ref, blk_ref,
                 q_ref, k_ref, v_ref, kb_ref, vb_ref, qg_ref, kg_ref, la_ref, vg_ref, rg_ref,
                 gt_ref):
    h = _rms(x_ref[...], g_ref[...]).astype(BF16)

    def head_norm(t, gain):
        ss = _dot((t * t).astype(BF16), blk_ref[...])
        return t * lax.rsqrt(ss * (1.0 / DH_SB) + EPS) * gain

    zsb = _dot(h, wsb_ref[...])
    q = head_norm(zsb[:, :W_SB], qn_ref[...])
    k = head_norm(zsb[:, W_SB:2 * W_SB], kn_ref[...])
    v = zsb[:, 2 * W_SB:]
    q_ref[...] = (q * (DH_SB ** -0.5)).astype(BF16)
    k_ref[...] = k
    v_ref[...] = v
    kb_ref[...] = k.astype(BF16)
    vb_ref[...] = v.astype(BF16)

    zgl = _dot(h, wgl_ref[...])
    qg_ref[...] = zgl[:, :WK_GLA] * (DK_GLA ** -0.5)
    kg_ref[...] = zgl[:, WK_GLA:2 * WK_GLA]
    vg_ref[...] = zgl[:, 2 * WK_GLA:2 * WK_GLA + WV_GLA]
    rg_ref[...] = zgl[:, 2 * WK_GLA + WV_GLA:]

    alr = _dot(h, walr_ref[...])
    lin = _dot(alr.astype(BF16), wa2_ref[...]) + ba_ref[...]
    la_ref[...] = _neg_softplus(-lin) * (1.0 / GATE_TAU)

    gt_ref[...] = jax.nn.sigmoid(_dot(h, wgt_ref[...]) + bgt_ref[...]).astype(BF16)


def _inproj(x, w, tm):
    t, d = x.shape
    row = lambda n: pl.BlockSpec((tm, n), lambda i: (i, 0))
    ins = [x, w["mix_norm"], w["w_sb"], w["w_gl"], w["w_alr"], w["w_a2"], w["b_a"], w["w_gt"],
           w["b_gt"], w["q_norm"], w["k_norm"], w["blk"]]
    in_specs = [row(d)] + [_full(a.shape) for a in ins[1:]]
    outs = [(W_SB, BF16), (W_SB, F32), (W_SB, F32), (W_SB, BF16), (W_SB, BF16),
            (WK_GLA, F32), (WK_GLA, F32), (WK_GLA, F32), (WV_GLA, F32), (WV_GLA, F32),
            (2 * d, BF16)]
    return pl.pallas_call(
        _inproj_body,
        grid=(t // tm,),
        in_specs=in_specs,
        out_specs=[row(n) for n, _ in outs],
        out_shape=[jax.ShapeDtypeStruct((t, n), dt) for n, dt in outs],
        compiler_params=_params("parallel"),
        name="inproj",
    )(*ins)


def _sb_prompt_body(bias_ref, q_ref, k_ref, v_ref, tri_ref, o_ref):
    pair = pl.program_id(1)
    qi = pl.program_id(2)
    tq = q_ref.shape[0]
    tk = tri_ref.shape[0]
    assert tq == 2 * tk
    lane = lax.broadcasted_iota(jnp.int32, (1, LANES), 1)
    head_lo = lane < DH_SB
    q2 = q_ref[...]
    zero = jnp.zeros_like(q2)
    qs = jnp.concatenate([jnp.where(head_lo, q2, zero), jnp.where(head_lo, zero, q2)], axis=0)
    top = lax.broadcasted_iota(jnp.int32, (2 * tq, 1), 0) < tq
    bias = jnp.where(top, bias_ref[2 * pair], bias_ref[2 * pair + 1])
    tri = tri_ref[...]
    t_pos = lax.broadcasted_iota(jnp.int32, (2 * tq, tk), 0) & (tq - 1)
    s_pos = lax.broadcasted_iota(jnp.int32, (2 * tq, tk), 1)

    def block(start, carry, acc, mask):
        start = pl.multiple_of(start, tk)
        kblk = k_ref[pl.ds(start, tk), :]
        vblk = v_ref[pl.ds(start, tk), :]
        z = _dot_nt(qs, kblk) + bias
        sp = jnp.maximum(z, 0.0) + jnp.log(1.0 + jnp.exp(-jnp.abs(z)))
        if mask is not None:
            sp = jnp.where(mask, sp, 0.0)
        later = _dot(sp.astype(BF16), tri)
        att = jnp.exp(z - sp - later - carry)
        if mask is not None:
            att = jnp.where(mask, att, 0.0)
        acc = acc + _dot(att.astype(BF16), vblk)
        carry = carry + jnp.sum(sp, axis=1, keepdims=True)
        return carry, acc

    q0 = qi * tq
    carry = jnp.zeros((2 * tq, 1), F32)
    acc = jnp.zeros((2 * tq, LANES), F32)
    carry, acc = block(q0 + tk, carry, acc, s_pos + tk < t_pos)
    carry, acc = block(q0, carry, acc, s_pos < t_pos)

    def two_tiles(i, c):
        c = block(q0 - (2 * i + 1) * tk, c[0], c[1], None)
        return block(q0 - (2 * i + 2) * tk, c[0], c[1], None)

    carry, acc = lax.fori_loop(0, qi, two_tiles, (carry, acc))
    o_ref[...] = jnp.where(head_lo, acc[:tq], acc[tq:]).astype(o_ref.dtype)


def _sb_prompt(q, k, v, bias, tri):
    b, s, w = q.shape
    tq = 2 * tri.shape[0]
    qspec = pl.BlockSpec((None, tq, LANES), lambda bi, p, i, *_: (bi, i, p))
    kvspec = pl.BlockSpec((None, s, LANES), lambda bi, p, i, *_: (bi, 0, p))
    return pl.pallas_call(
        _sb_prompt_body,
        grid_spec=pltpu.PrefetchScalarGridSpec(
            num_scalar_prefetch=1,
            grid=(b, w // LANES, s // tq),
            in_specs=[qspec, kvspec, kvspec, pl.BlockSpec(tri.shape, lambda *_: (0, 0))],
            out_specs=qspec,
        ),
        out_shape=jax.ShapeDtypeStruct((b, s, w), BF16),
        compiler_params=_params("parallel", "parallel", "arbitrary"),
        name="sb_prompt",
    )(bias, q, k, v, tri)


def _gla_prompt_body(q_ref, k_ref, la_ref, v_ref, r_ref, gn_ref, tril_ref, o_ref, st_ref, stt_ref):
    c = pl.program_id(0)
    nb = q_ref.shape[0]
    cs = q_ref.shape[1]

    @pl.when(c == 0)
    def _():
        stt_ref[...] = jnp.zeros_like(stt_ref)

    lane = lax.broadcasted_iota(jnp.int32, (1, LANES), 1)
    head_lo = lane < DK_GLA
    ti = lax.broadcasted_iota(jnp.int32, (cs, cs), 0)
    tj = lax.broadcasted_iota(jnp.int32, (cs, cs), 1)
    causal = tj <= ti
    tril = tril_ref[...]
    gn = gn_ref[...]

    def per_batch(b, _):
        la = la_ref[b]
        la_hi, la_lo = _split_bf16(la)
        bc = _dot(tril, la_hi) + _dot(tril, la_lo)
        b_last = bc[cs - 1:cs, :]
        q_dec = q_ref[b] * jnp.exp(bc)
        k_inv = (k_ref[b] * jnp.exp(-bc)).astype(BF16)
        k_end = (k_ref[b] * jnp.exp(b_last - bc)).astype(BF16)
        decay = jnp.exp(b_last)
        vb = v_ref[b].astype(BF16)
        rb = r_ref[b]
        for p in range(H_GLA // 2):
            ps = slice(p * LANES, (p + 1) * LANES)
            qd2 = q_dec[:, ps]
            ki2 = k_inv[:, ps]
            ke2 = k_end[:, ps]
            stt = stt_ref[b, p]
            stt_b = stt.astype(BF16)
            kvs = []
            for hh in range(2):
                h = 2 * p + hh
                hs = slice(h * DV_GLA, (h + 1) * DV_GLA)
                in_head = head_lo if hh == 0 else jnp.logical_not(head_lo)
                qa = jnp.where(in_head, qd2, 0.0).astype(BF16)
                att = jnp.where(causal, _dot_nt(qa, ki2), 0.0)
                vh = vb[:, hs]
                o = _dot(att.astype(BF16), vh) + _dot_nt(qa, stt_b)
                o = _rms(o, gn)
                r = rb[:, hs]
                o_ref[b, :, hs] = (o * (r * jax.nn.sigmoid(r))).astype(o_ref.dtype)
                kvs.append(_dot_tn(vh, ke2))
            stt_ref[b, p] = decay[:, ps] * stt + jnp.where(head_lo, kvs[0], kvs[1])
        return 0

    lax.fori_loop(0, nb, per_batch, 0)

    @pl.when(c == pl.num_programs(0) - 1)
    def _():
        def finish(b, _):
            for p in range(H_GLA // 2):
                st_ref[b, p] = stt_ref[b, p].T
            return 0
        lax.fori_loop(0, nb, finish, 0)


def _gla_prompt(qg, kg, la, vg, rg, gn, tril):
    b, s, _ = qg.shape
    cs = min(GLA_CHUNK, s)
    assert s % cs == 0
    chunk = lambda n: pl.BlockSpec((b, cs, n), lambda c: (0, c, 0))
    o, st = pl.pallas_call(
        _gla_prompt_body,
        grid=(s // cs,),
        in_specs=[chunk(WK_GLA), chunk(WK_GLA), chunk(WK_GLA), chunk(WV_GLA), chunk(WV_GLA),
                  _full(gn.shape), _full(tril.shape)],
        out_specs=[chunk(WV_GLA), _full((b, H_GLA // 2, 2 * DK_GLA, DV_GLA))],
        out_shape=[jax.ShapeDtypeStruct((b, s, WV_GLA), BF16),
                   jax.ShapeDtypeStruct((b, H_GLA // 2, 2 * DK_GLA, DV_GLA), F32)],
        scratch_shapes=[pltpu.VMEM((b, H_GLA // 2, DV_GLA, 2 * DK_GLA), F32)],
        compiler_params=_params("arbitrary"),
        name="gla_prompt",
    )(qg, kg, la, vg, rg, gn, tril)
    return o, st.reshape(b, H_GLA, DK_GLA, DV_GLA)


def _merge_body(x_ref, osb_ref, og_ref, gt_ref, wosb_ref, wogl_ref, wout_ref, o_ref):
    d = x_ref.shape[1]
    g = gt_ref[...].astype(F32)
    m = g[:, :d] * _dot(osb_ref[...], wosb_ref[...]) + g[:, d:] * _dot(og_ref[...], wogl_ref[...])
    o_ref[...] = x_ref[...] + _dot(m.astype(BF16), wout_ref[...])


def _merge(x, osb, og, gt, w, tm):
    t, d = x.shape
    row = lambda n: pl.BlockSpec((tm, n), lambda i: (i, 0))
    return pl.pallas_call(
        _merge_body,
        grid=(t // tm,),
        in_specs=[row(d), row(W_SB), row(WV_GLA), row(2 * d),
                  _full(w["w_o_sb"].shape), _full(w["w_o_gla"].shape), _full(w["w_out"].shape)],
        out_specs=row(d),
        out_shape=jax.ShapeDtypeStruct((t, d), F32),
        compiler_params=_params("parallel"),
        name="merge",
    )(x, osb, og, gt, w["w_o_sb"], w["w_o_gla"], w["w_out"])


def _sb_decode_body(pt_ref, q_ref, kn_ref, vn_ref, bias_ref, tri_ref, *rest, pages, past_len):
    k_refs = rest[:pages]
    v_refs = rest[pages:2 * pages]
    o_ref, acc_ref, carry_ref = rest[2 * pages:]
    j = pl.program_id(1)
    page = k_refs[0].shape[0]
    rows = page * H_SB
    per_page = rows // LANES
    n_chunks = pages * per_page
    sub = lax.broadcasted_iota(jnp.int32, (H_SB, LANES), 0)
    lane = lax.broadcasted_iota(jnp.int32, (H_SB, LANES), 1)
    own = (lane & (H_SB - 1)) == sub
    q = q_ref[...]
    qb = q.astype(BF16)
    bias = bias_ref[...]
    tri = tri_ref[...]

    @pl.when(j == 0)
    def _():
        s_pos = past_len + lax.broadcasted_iota(jnp.int32, (H_SB, 1), 1)
        visible = s_pos < past_len
        z = jnp.sum(q * kn_ref[...], axis=1, keepdims=True) + bias[:, :1]
        l1m = jnp.where(visible, _neg_softplus(z), 0.0)
        att = jnp.where(visible, jnp.exp(z + l1m), 0.0)
        acc_ref[...] = att * vn_ref[...]
        carry_ref[...] = jnp.broadcast_to(l1m, carry_ref.shape)

    zs = []
    for p in range(pages):
        k2 = k_refs[p][...].reshape(rows, DH_SB).astype(BF16)
        z = _dot_nt(qb, k2)
        zs += [z[:, c * LANES:(c + 1) * LANES] for c in reversed(range(per_page))]
    z3 = jnp.stack(zs) + bias[None]
    l1m = jnp.where(own[None], _neg_softplus(z3), 0.0)
    hi, lo = _split_bf16(l1m.reshape(n_chunks * H_SB, LANES))
    within = (_dot(hi, tri) + _dot(lo, tri)).reshape(n_chunks, H_SB, LANES)
    chunk_sum = jnp.sum(l1m, axis=2, keepdims=True)
    carry = carry_ref[:, :1]
    carries = []
    for g in range(n_chunks):
        carries.append(carry)
        carry = carry + chunk_sum[g]
    att = jnp.exp(z3 + l1m + within + jnp.stack(carries))
    att = jnp.where(own[None], att, 0.0).astype(BF16)
    acc = acc_ref[...]
    for p in range(pages):
        att_p = jnp.concatenate(
            [att[p * per_page + (per_page - 1 - c)] for c in range(per_page)], axis=1)
        v2 = v_refs[p][...].reshape(rows, DH_SB).astype(BF16)
        acc = acc + _dot(att_p, v2)
    acc_ref[...] = acc
    carry_ref[...] = jnp.broadcast_to(carry, carry_ref.shape)

    @pl.when(j == pl.num_programs(1) - 1)
    def _():
        o_ref[...] = acc


def _sb_decode(q, k_new, v_new, cache_k, cache_v, pool_offset, page_table, bias8, tri):
    nb, n_pages = page_table.shape
    page = LANES
    pages = DEC_PAGES_PER_STEP
    assert n_pages % pages == 0 and cache_k.shape[0] % page == 0
    steps = n_pages // pages
    past_len = n_pages * page

    def page_spec(p):
        return pl.BlockSpec(
            (page, H_SB, DH_SB),
            lambda b, j, pt: (pool_offset + pt[b, n_pages - 1 - (j * pages + p)], 0, 0))

    vec = pl.BlockSpec((None, H_SB, DH_SB), lambda b, j, pt: (b, 0, 0))
    const = lambda shape: pl.BlockSpec(shape, lambda b, j, pt: (0,) * len(shape))
    return pl.pallas_call(
        functools.partial(_sb_decode_body, pages=pages, past_len=past_len),
        grid_spec=pltpu.PrefetchScalarGridSpec(
            num_scalar_prefetch=1,
            grid=(nb, steps),
            in_specs=[vec, vec, vec, const(bias8.shape), const(tri.shape)]
            + [page_spec(p) for p in range(pages)] + [page_spec(p) for p in range(pages)],
            out_specs=vec,
            scratch_shapes=[pltpu.VMEM((H_SB, DH_SB), F32), pltpu.VMEM((H_SB, LANES), F32)],
        ),
        out_shape=jax.ShapeDtypeStruct((nb, H_SB, DH_SB), F32),
        compiler_params=_params("parallel", "arbitrary"),
        name="sb_decode",
    )(page_table, q, k_new, v_new, bias8, tri, *([cache_k] * pages), *([cache_v] * pages))


def _gla_step_body(q_ref, k_ref, la_ref, v_ref, r_ref, gn_ref, s0_ref, o_ref, s1_ref):
    nb = q_ref.shape[0]
    rowid = lax.broadcasted_iota(jnp.int32, (8, WK_GLA), 0)
    own = rowid == lax.broadcasted_iota(jnp.int32, (8, WK_GLA), 1) // DK_GLA
    vrow = lax.broadcasted_iota(jnp.int32, (8, DV_GLA), 0)
    ones = jnp.ones((8, DV_GLA), BF16)
    gn = gn_ref[...]

    def rows_of_heads(x):
        out = jnp.broadcast_to(x[:, :DV_GLA], (8, DV_GLA))
        for h in range(1, H_GLA):
            out = jnp.where(vrow == h, jnp.broadcast_to(x[:, h * DV_GLA:(h + 1) * DV_GLA],
                                                        (8, DV_GLA)), out)
        return out

    for i in range(nb):
        q = q_ref[i:i + 1, :]
        k = k_ref[i:i + 1, :]
        decay = jnp.exp(la_ref[i:i + 1, :])
        s0 = s0_ref[i]
        v8 = rows_of_heads(v_ref[i:i + 1, :])
        r8 = rows_of_heads(r_ref[i:i + 1, :])
        spread = lambda x: jnp.where(own, jnp.broadcast_to(x, (8, WK_GLA)), 0.0)
        qk = jnp.sum(spread(q * k), axis=1, keepdims=True)
        o = qk * v8 + _dot(spread(q * decay).astype(BF16), s0.astype(BF16))
        o = _rms(o, gn) * (r8 * jax.nn.sigmoid(r8))
        for h in range(H_GLA):
            o_ref[i:i + 1, h * DV_GLA:(h + 1) * DV_GLA] = o[h:h + 1, :].astype(o_ref.dtype)
        d_hi = decay.astype(BF16).astype(F32)
        dd = jnp.where(rowid == 0, jnp.broadcast_to(d_hi, (8, WK_GLA)),
                       jnp.where(rowid == 1, jnp.broadcast_to(decay - d_hi, (8, WK_GLA)), 0.0))
        decay_col = _dot_tn(dd.astype(BF16), ones)
        s1_ref[i] = decay_col * s0 + _dot_tn(spread(k).astype(BF16), v8.astype(BF16))


def _gla_step(qg, kg, la, vg, rg, gn, s0, layer, nb_blk=8):
    nb = qg.shape[0]
    blk0 = layer * (nb // nb_blk)
    row = lambda n: pl.BlockSpec((nb_blk, n), lambda i: (i, 0))
    st = pl.BlockSpec((nb_blk, WK_GLA, DV_GLA), lambda i: (i, 0, 0))
    st_in = pl.BlockSpec((nb_blk, WK_GLA, DV_GLA), lambda i: (blk0 + i, 0, 0))
    o, s1 = pl.pallas_call(
        _gla_step_body,
        grid=(nb // nb_blk,),
        in_specs=[row(WK_GLA), row(WK_GLA), row(WK_GLA), row(WV_GLA), row(WV_GLA),
                  _full(gn.shape), st_in],
        out_specs=[row(WV_GLA), st],
        out_shape=[jax.ShapeDtypeStruct((nb, WV_GLA), BF16),
                   jax.ShapeDtypeStruct((nb, WK_GLA, DV_GLA), F32)],
        compiler_params=_params("parallel"),
        name="gla_step",
    )(qg, kg, la, vg, rg, gn, s0)
    return o, s1.reshape(nb, H_GLA, DK_GLA, DV_GLA)


def _prep_layer(l, ffn1_norm, ffn1_w_in, ffn1_w_out, mix_norm, w_in, b_gate, q_norm, k_norm,
                sb_bias, w_a2, b_a, gla_norm, w_o_sb, w_o_gla, w_out, ffn2_norm, ffn2_w_in,
                ffn2_w_out):
    d = w_in.shape[1]
    d_ff = ffn1_w_out.shape[1]
    bf = lambda a: a.astype(BF16)
    o_gl = 3 * W_SB
    o_alr = o_gl + 2 * WK_GLA + 2 * WV_GLA
    o_gt = o_alr + GATE_RANK
    win = w_in[l]
    head = jnp.arange(W_SB) // DH_SB
    return {
        "ffn1": (ffn1_norm[l][None], bf(ffn1_w_in[l][:, :d_ff]), bf(ffn1_w_in[l][:, d_ff:]),
                 bf(ffn1_w_out[l])),
        "ffn2": (ffn2_norm[l][None], bf(ffn2_w_in[l][:, :d_ff]), bf(ffn2_w_in[l][:, d_ff:]),
                 bf(ffn2_w_out[l])),
        "mix_norm": mix_norm[l][None],
        "w_sb": bf(win[:, :o_gl]),
        "w_gl": bf(win[:, o_gl:o_alr]),
        "w_alr": bf(jnp.pad(win[:, o_alr:o_gt], ((0, 0), (0, LANES - GATE_RANK)))),
        "w_a2": bf(jnp.pad(w_a2[l], ((0, LANES - GATE_RANK), (0, 0)))),
        "b_a": b_a[l][None],
        "w_gt": bf(win[:, o_gt:]),
        "b_gt": b_gate[l][None],
        "q_norm": jnp.tile(q_norm[l], H_SB)[None],
        "k_norm": jnp.tile(k_norm[l], H_SB)[None],
        "blk": (head[:, None] == head[None, :]).astype(BF16),
        "sb_bias": sb_bias[l],
        "gla_norm": gla_norm[l][None],
        "w_o_sb": bf(w_o_sb[l]),
        "w_o_gla": bf(w_o_gla[l]),
        "w_out": bf(w_out[l]),
    }


def _strict_upper(n):
    i = jnp.arange(n)
    return (i[:, None] > i[None, :]).astype(BF16)


def _prompt_layer(x, w):
    b, s, d = x.shape
    tm = 512
    x2 = x.reshape(b * s, d)
    x2 = _ffn(x2, *w["ffn1"], tm)
    q, k, v, kb, vb, qg, kg, la, vg, rg, gt = _inproj(x2, w, tm)
    sh = lambda a: a.reshape(b, s, a.shape[-1])
    osb = _sb_prompt(sh(q), sh(kb), sh(vb), w["sb_bias"], _strict_upper(SB_TILE))
    cs = min(GLA_CHUNK, s)
    i = jnp.arange(cs)
    tril = (i[None, :] <= i[:, None]).astype(BF16)
    og, st = _gla_prompt(sh(qg), sh(kg), sh(la), sh(vg), sh(rg), w["gla_norm"], tril)
    x2 = _merge(x2, osb.reshape(b * s, W_SB), og.reshape(b * s, WV_GLA), gt, w, tm)
    x2 = _ffn(x2, *w["ffn2"], tm)
    return (x2.reshape(b, s, d), k.reshape(b, s, H_SB, DH_SB), v.reshape(b, s, H_SB, DH_SB), st)


def _sample_layer(x, cache_k, cache_v, state, page_table, layer, w):
    b, s, d = x.shape
    assert s == 1
    tm = b
    x2 = x.reshape(b, d)
    x2 = _ffn(x2, *w["ffn1"], tm)
    q, k, v, _, _, qg, kg, la, vg, rg, gt = _inproj(x2, w, tm)
    depth, n_pool, page = cache_k.shape[:3]
    assert page == LANES
    bias8 = jnp.broadcast_to(w["sb_bias"][:, None], (H_SB, LANES))
    tok = jnp.arange(LANES) // H_SB
    heads = lambda a: a.astype(F32).reshape(b, H_SB, DH_SB)
    osb = _sb_decode(heads(q), heads(k), heads(v),
                     cache_k.reshape(depth * n_pool * page, H_SB, DH_SB),
                     cache_v.reshape(depth * n_pool * page, H_SB, DH_SB),
                     layer * n_pool, page_table, bias8,
                     (tok[:, None] > tok[None, :]).astype(BF16))
    osb = osb.reshape(b, W_SB).astype(BF16)
    og, st = _gla_step(qg, kg, la, vg, rg, w["gla_norm"],
                       state.reshape(depth * b, WK_GLA, DV_GLA), layer)
    x2 = _merge(x2, osb.reshape(b, W_SB), og, gt, w, tm)
    x2 = _ffn(x2, *w["ffn2"], tm)
    return (x2.reshape(b, s, d), k.reshape(b, s, H_SB, DH_SB), v.reshape(b, s, H_SB, DH_SB), st)


def kernel(x_prompt, x_sample, cache_k, cache_v, state_gla, page_table, ffn1_norm, ffn1_w_in,
           ffn1_w_out, mix_norm, w_in, b_gate, q_norm, k_norm, sb_bias, w_a2, b_a, gla_norm,
           w_o_sb, w_o_gla, w_out, ffn2_norm, ffn2_w_in, ffn2_w_out):
    depth = w_in.shape[0]
    yp, ys = x_prompt, x_sample
    outs = [[] for _ in range(6)]
    for l in range(depth):
        w = _prep_layer(l, ffn1_norm, ffn1_w_in, ffn1_w_out, mix_norm, w_in, b_gate, q_norm,
                        k_norm, sb_bias, w_a2, b_a, gla_norm, w_o_sb, w_o_gla, w_out, ffn2_norm,
                        ffn2_w_in, ffn2_w_out)
        yp, kp, vp, sp = _prompt_layer(yp, w)
        ys, kd, vd, sd = _sample_layer(ys, cache_k, cache_v, state_gla, page_table, l, w)
        for acc, val in zip(outs, (kp, vp, sp, kd, vd, sd)):
            acc.append(val)
    return (yp, ys) + tuple(jnp.stack(o) for o in outs)
```

```python
import functools

import jax
import jax.numpy as jnp
from jax import lax
from jax.experimental import pallas as pl
from jax.experimental.pallas import tpu as pltpu

F32 = jnp.float32
BF16 = jnp.bfloat16

EPS = 1e-6
H_SB = 8
DH_SB = 64
W_SB = H_SB * DH_SB
H_GLA = 4
DK_GLA = 64
DV_GLA = 128
WK_GLA = H_GLA * DK_GLA
WV_GLA = H_GLA * DV_GLA
GATE_RANK = 16
GATE_TAU = 16.0
GLA_CHUNK = 64

LANES = 128
VMEM_LIMIT_BYTES = 56 * 1024 * 1024
SB_TILE = 256
DEC_PAGES_PER_STEP = 16

_NT = (((1,), (1,)), ((), ()))
_TN = (((0,), (0,)), ((), ()))


def _dot(a, b):
    return jnp.dot(a, b, preferred_element_type=F32)


def _dot_nt(a, b):
    return lax.dot_general(a, b, _NT, preferred_element_type=F32)


def _dot_tn(a, b):
    return lax.dot_general(a, b, _TN, preferred_element_type=F32)


def _rms(x, g):
    return x * lax.rsqrt(jnp.mean(x * x, axis=-1, keepdims=True) + EPS) * g


def _neg_softplus(z):
    return -(jnp.maximum(z, 0.0) + jnp.log1p(jnp.exp(-jnp.abs(z))))


def _split_bf16(x):
    hi = x.astype(BF16)
    lo = (x - hi.astype(F32)).astype(BF16)
    return hi, lo


def _params(*sem):
    return pltpu.CompilerParams(dimension_semantics=sem, vmem_limit_bytes=VMEM_LIMIT_BYTES)


def _full(shape):
    nd = len(shape)
    return pl.BlockSpec(shape, lambda *_: (0,) * nd)


def _ffn_body(x_ref, g_ref, wa_ref, wb_ref, wo_ref, o_ref, *, chunks):
    x = x_ref[...]
    xn = _rms(x, g_ref[...]).astype(BF16)
    acc = None
    for c0, cw in chunks:
        a = _dot(xn, wa_ref[:, c0:c0 + cw])
        b = _dot(xn, wb_ref[:, c0:c0 + cw])
        h = (a * jax.nn.sigmoid(a) * b).astype(BF16)
        y = _dot(h, wo_ref[c0:c0 + cw, :])
        acc = y if acc is None else acc + y
    o_ref[...] = x + 0.5 * acc


def _ffn_chunks(f, width=1024):
    assert f % 256 == 0
    return tuple((c0, min(width, f - c0)) for c0 in range(0, f, width))


def _ffn(x, g, wa, wb, wo, tm):
    t, d = x.shape
    f = wa.shape[1]
    chunks = _ffn_chunks(f)
    row = pl.BlockSpec((tm, d), lambda i: (i, 0))
    return pl.pallas_call(
        functools.partial(_ffn_body, chunks=chunks),
        grid=(t // tm,),
        in_specs=[row, _full((1, d)), _full((d, f)), _full((d, f)), _full((f, d))],
        out_specs=row,
        out_shape=jax.ShapeDtypeStruct((t, d), F32),
        compiler_params=_params("parallel"),
        name="ffn",
    )(x, g, wa, wb, wo)


def _inproj_body(x_ref, g_ref, wsb_ref, wgl_ref, walr_ref, wa2_ref, ba_ref, wgt_ref, bgt_ref,
                 qn_ref, kn_ref, blk_ref,
                 q_ref, k_ref, v_ref, kb_ref, vb_ref, qg_ref, kg_ref, la_ref, vg_ref, rg_ref,
                 gt_ref, *, kv_transposed):
    h = _rms(x_ref[...], g_ref[...]).astype(BF16)

    def head_norm(t, gain):
        ss = _dot((t * t).astype(BF16), blk_ref[...])
        return t * lax.rsqrt(ss * (1.0 / DH_SB) + EPS) * gain

    zsb = _dot(h, wsb_ref[...])
    q = head_norm(zsb[:, :W_SB], qn_ref[...])
    k = head_norm(zsb[:, W_SB:2 * W_SB], kn_ref[...])
    v = zsb[:, 2 * W_SB:]
    q_ref[...] = (q * (DH_SB ** -0.5)).astype(BF16)
    k_ref[...] = k.T if kv_transposed else k
    v_ref[...] = v.T if kv_transposed else v
    kb_ref[...] = k.astype(BF16)
    vb_ref[...] = v.astype(BF16)

    zgl = _dot(h, wgl_ref[...])
    qg_ref[...] = zgl[:, :WK_GLA] * (DK_GLA ** -0.5)
    kg_ref[...] = zgl[:, WK_GLA:2 * WK_GLA]
    vg_ref[...] = zgl[:, 2 * WK_GLA:2 * WK_GLA + WV_GLA]
    rg_ref[...] = zgl[:, 2 * WK_GLA + WV_GLA:]

    alr = _dot(h, walr_ref[...])
    lin = _dot(alr.astype(BF16), wa2_ref[...]) + ba_ref[...]
    la_ref[...] = _neg_softplus(-lin) * (1.0 / GATE_TAU)

    gt_ref[...] = jax.nn.sigmoid(_dot(h, wgt_ref[...]) + bgt_ref[...]).astype(BF16)


def _inproj(x, w, tm, seq=None):
    t, d = x.shape
    row = lambda n: pl.BlockSpec((tm, n), lambda i: (i, 0))
    ins = [x, w["mix_norm"], w["w_sb"], w["w_gl"], w["w_alr"], w["w_a2"], w["b_a"], w["w_gt"],
           w["b_gt"], w["q_norm"], w["k_norm"], w["blk"]]
    in_specs = [row(d)] + [_full(a.shape) for a in ins[1:]]
    outs = [(W_SB, BF16), (W_SB, F32), (W_SB, F32), (W_SB, BF16), (W_SB, BF16),
            (WK_GLA, F32), (WK_GLA, F32), (WK_GLA, F32), (WV_GLA, F32), (WV_GLA, F32),
            (2 * d, BF16)]
    out_specs = [row(n) for n, _ in outs]
    out_shape = [jax.ShapeDtypeStruct((t, n), dt) for n, dt in outs]
    if seq is not None:
        per_seq = seq // tm
        kv_t = pl.BlockSpec((None, W_SB, tm), lambda i: (i // per_seq, 0, i % per_seq))
        out_specs[1] = out_specs[2] = kv_t
        out_shape[1] = out_shape[2] = jax.ShapeDtypeStruct((t // seq, W_SB, seq), F32)
    return pl.pallas_call(
        functools.partial(_inproj_body, kv_transposed=seq is not None),
        grid=(t // tm,),
        in_specs=in_specs,
        out_specs=out_specs,
        out_shape=out_shape,
        compiler_params=_params("parallel"),
        name="inproj",
    )(*ins)


def _sb_prompt_body(bias_ref, q_ref, k_ref, v_ref, tri_ref, o_ref):
    pair = pl.program_id(1)
    qi = pl.program_id(2)
    tq = q_ref.shape[0]
    tk = tri_ref.shape[0]
    assert tq == 2 * tk
    lane = lax.broadcasted_iota(jnp.int32, (1, LANES), 1)
    head_lo = lane < DH_SB
    q2 = q_ref[...]
    zero = jnp.zeros_like(q2)
    qs = jnp.concatenate([jnp.where(head_lo, q2, zero), jnp.where(head_lo, zero, q2)], axis=0)
    top = lax.broadcasted_iota(jnp.int32, (2 * tq, 1), 0) < tq
    bias = jnp.where(top, bias_ref[2 * pair], bias_ref[2 * pair + 1])
    tri = tri_ref[...]
    t_pos = lax.broadcasted_iota(jnp.int32, (2 * tq, tk), 0) & (tq - 1)
    s_pos = lax.broadcasted_iota(jnp.int32, (2 * tq, tk), 1)

    def block(start, carry, acc, mask):
        start = pl.multiple_of(start, tk)
        kblk = k_ref[pl.ds(start, tk), :]
        vblk = v_ref[pl.ds(start, tk), :]
        z = _dot_nt(qs, kblk) + bias
        sp = jnp.maximum(z, 0.0) + jnp.log(1.0 + jnp.exp(-jnp.abs(z)))
        if mask is not None:
            sp = jnp.where(mask, sp, 0.0)
        later = _dot(sp.astype(BF16), tri)
        att = jnp.exp(z - sp - later - carry)
        if mask is not None:
            att = jnp.where(mask, att, 0.0)
        acc = acc + _dot(att.astype(BF16), vblk)
        carry = carry + jnp.sum(sp, axis=1, keepdims=True)
        return carry, acc

    q0 = qi * tq
    carry = jnp.zeros((2 * tq, 1), F32)
    acc = jnp.zeros((2 * tq, LANES), F32)
    carry, acc = block(q0 + tk, carry, acc, s_pos + tk < t_pos)
    carry, acc = block(q0, carry, acc, s_pos < t_pos)

    def two_tiles(i, c):
        c = block(q0 - (2 * i + 1) * tk, c[0], c[1], None)
        return block(q0 - (2 * i + 2) * tk, c[0], c[1], None)

    carry, acc = lax.fori_loop(0, qi, two_tiles, (carry, acc))
    o_ref[...] = jnp.where(head_lo, acc[:tq], acc[tq:]).astype(o_ref.dtype)


def _sb_prompt(q, k, v, bias, tri):
    b, s, w = q.shape
    tq = 2 * tri.shape[0]
    qspec = pl.BlockSpec((None, tq, LANES), lambda bi, p, i, *_: (bi, i, p))
    kvspec = pl.BlockSpec((None, s, LANES), lambda bi, p, i, *_: (bi, 0, p))
    return pl.pallas_call(
        _sb_prompt_body,
        grid_spec=pltpu.PrefetchScalarGridSpec(
            num_scalar_prefetch=1,
            grid=(b, w // LANES, s // tq),
            in_specs=[qspec, kvspec, kvspec, pl.BlockSpec(tri.shape, lambda *_: (0, 0))],
            out_specs=qspec,
        ),
        out_shape=jax.ShapeDtypeStruct((b, s, w), BF16),
        compiler_params=_params("parallel", "parallel", "arbitrary"),
        name="sb_prompt",
    )(bias, q, k, v, tri)


def _gla_prompt_body(q_ref, k_ref, la_ref, v_ref, r_ref, gn_ref, tril_ref, o_ref, st_ref, stt_ref):
    c = pl.program_id(0)
    nb = q_ref.shape[0]
    cs = q_ref.shape[1]

    @pl.when(c == 0)
    def _():
        stt_ref[...] = jnp.zeros_like(stt_ref)

    lane = lax.broadcasted_iota(jnp.int32, (1, LANES), 1)
    head_lo = lane < DK_GLA
    ti = lax.broadcasted_iota(jnp.int32, (cs, cs), 0)
    tj = lax.broadcasted_iota(jnp.int32, (cs, cs), 1)
    causal = tj <= ti
    tril = tril_ref[...]
    gn = gn_ref[...]

    def per_batch(b, _):
        la = la_ref[b]
        la_hi, la_lo = _split_bf16(la)
        bc = _dot(tril, la_hi) + _dot(tril, la_lo)
        b_last = bc[cs - 1:cs, :]
        q_dec = q_ref[b] * jnp.exp(bc)
        k_inv = (k_ref[b] * jnp.exp(-bc)).astype(BF16)
        k_end = (k_ref[b] * jnp.exp(b_last - bc)).astype(BF16)
        decay = jnp.exp(b_last)
        vb = v_ref[b].astype(BF16)
        rb = r_ref[b]
        for p in range(H_GLA // 2):
            ps = slice(p * LANES, (p + 1) * LANES)
            qd2 = q_dec[:, ps]
            ki2 = k_inv[:, ps]
            ke2 = k_end[:, ps]
            stt = stt_ref[b, p]
            stt_b = stt.astype(BF16)
            kvs = []
            for hh in range(2):
                h = 2 * p + hh
                hs = slice(h * DV_GLA, (h + 1) * DV_GLA)
                in_head = head_lo if hh == 0 else jnp.logical_not(head_lo)
                qa = jnp.where(in_head, qd2, 0.0).astype(BF16)
                att = jnp.where(causal, _dot_nt(qa, ki2), 0.0)
                vh = vb[:, hs]
                o = _dot(att.astype(BF16), vh) + _dot_nt(qa, stt_b)
                o = _rms(o, gn)
                r = rb[:, hs]
                o_ref[b, :, hs] = (o * (r * jax.nn.sigmoid(r))).astype(o_ref.dtype)
                kvs.append(_dot_tn(vh, ke2))
            stt_ref[b, p] = decay[:, ps] * stt + jnp.where(head_lo, kvs[0], kvs[1])
        return 0

    lax.fori_loop(0, nb, per_batch, 0, unroll=2)

    @pl.when(c == pl.num_programs(0) - 1)
    def _():
        def finish(b, _):
            for p in range(H_GLA // 2):
                st_ref[b, p] = stt_ref[b, p].T
            return 0
        lax.fori_loop(0, nb, finish, 0)


def _gla_prompt(qg, kg, la, vg, rg, gn, tril):
    b, s, _ = qg.shape
    cs = min(GLA_CHUNK, s)
    assert s % cs == 0
    chunk = lambda n: pl.BlockSpec((b, cs, n), lambda c: (0, c, 0))
    o, st = pl.pallas_call(
        _gla_prompt_body,
        grid=(s // cs,),
        in_specs=[chunk(WK_GLA), chunk(WK_GLA), chunk(WK_GLA), chunk(WV_GLA), chunk(WV_GLA),
                  _full(gn.shape), _full(tril.shape)],
        out_specs=[chunk(WV_GLA), _full((b, H_GLA // 2, 2 * DK_GLA, DV_GLA))],
        out_shape=[jax.ShapeDtypeStruct((b, s, WV_GLA), BF16),
                   jax.ShapeDtypeStruct((b, H_GLA // 2, 2 * DK_GLA, DV_GLA), F32)],
        scratch_shapes=[pltpu.VMEM((b, H_GLA // 2, DV_GLA, 2 * DK_GLA), F32)],
        compiler_params=_params("arbitrary"),
        name="gla_prompt",
    )(qg, kg, la, vg, rg, gn, tril)
    return o, st.reshape(b, H_GLA, DK_GLA, DV_GLA)


def _merge_body(x_ref, osb_ref, og_ref, gt_ref, wosb_ref, wogl_ref, wout_ref, o_ref):
    d = x_ref.shape[1]
    g = gt_ref[...].astype(F32)
    m = g[:, :d] * _dot(osb_ref[...], wosb_ref[...]) + g[:, d:] * _dot(og_ref[...], wogl_ref[...])
    o_ref[...] = x_ref[...] + _dot(m.astype(BF16), wout_ref[...])


def _merge(x, osb, og, gt, w, tm):
    t, d = x.shape
    row = lambda n: pl.BlockSpec((tm, n), lambda i: (i, 0))
    return pl.pallas_call(
        _merge_body,
        grid=(t // tm,),
        in_specs=[row(d), row(W_SB), row(WV_GLA), row(2 * d),
                  _full(w["w_o_sb"].shape), _full(w["w_o_gla"].shape), _full(w["w_out"].shape)],
        out_specs=row(d),
        out_shape=jax.ShapeDtypeStruct((t, d), F32),
        compiler_params=_params("parallel"),
        name="merge",
    )(x, osb, og, gt, w["w_o_sb"], w["w_o_gla"], w["w_out"])


def _sb_decode_body(pt_ref, q_ref, kn_ref, vn_ref, bias_ref, tri_ref, *rest, pages, past_len):
    k_refs = rest[:pages]
    v_refs = rest[pages:2 * pages]
    o_ref, acc_ref, carry_ref = rest[2 * pages:]
    j = pl.program_id(1)
    w = q_ref.shape[1]
    rowid = lax.broadcasted_iota(jnp.int32, (H_SB, w), 0)
    own = rowid == lax.broadcasted_iota(jnp.int32, (H_SB, w), 1) // DH_SB
    qbd = jnp.where(own, jnp.broadcast_to(q_ref[...].astype(F32), (H_SB, w)), 0.0)
    qb = qbd.astype(BF16)
    bias = bias_ref[...]
    tri = tri_ref[...]

    @pl.when(j == 0)
    def _():
        s_pos = past_len + lax.broadcasted_iota(jnp.int32, (H_SB, 1), 1)
        visible = s_pos < past_len
        z = jnp.sum(qbd * kn_ref[...], axis=1, keepdims=True) + bias[:, :1]
        l1m = jnp.where(visible, _neg_softplus(z), 0.0)
        att = jnp.where(visible, jnp.exp(z + l1m), 0.0)
        acc_ref[...] = att * jnp.broadcast_to(vn_ref[...], (H_SB, w))
        carry_ref[...] = jnp.broadcast_to(l1m, carry_ref.shape)

    z3 = jnp.stack([_dot(qb, k_refs[p][...].astype(BF16)) for p in range(pages)]) + bias[None]
    l1m = _neg_softplus(z3)
    hi, lo = _split_bf16(l1m.reshape(pages * H_SB, LANES))
    within = (_dot(hi, tri) + _dot(lo, tri)).reshape(pages, H_SB, LANES)
    page_sum = jnp.sum(l1m, axis=2, keepdims=True)
    carry = carry_ref[:, :1]
    carries = []
    for p in range(pages):
        carries.append(carry)
        carry = carry + page_sum[p]
    att = jnp.exp(z3 + l1m + within + jnp.stack(carries)).astype(BF16)
    acc = acc_ref[...]
    for p in range(pages):
        acc = acc + _dot_nt(att[p], v_refs[p][...].astype(BF16))
    acc_ref[...] = acc
    carry_ref[...] = jnp.broadcast_to(carry, carry_ref.shape)

    @pl.when(j == pl.num_programs(1) - 1)
    def _():
        o_ref[...] = jnp.sum(jnp.where(own, acc, 0.0), axis=0, keepdims=True).astype(o_ref.dtype)


def _sb_decode(q, k_new, v_new, cache_kt, cache_vt, pool_offset, page_table, bias8, tri):
    nb, n_pages = page_table.shape
    _, w, page = cache_kt.shape
    pages = DEC_PAGES_PER_STEP
    assert n_pages % pages == 0 and page == LANES
    steps = n_pages // pages
    past_len = n_pages * page

    def page_spec(p):
        return pl.BlockSpec(
            (None, w, page),
            lambda b, j, pt: (pool_offset + pt[b, n_pages - 1 - (j * pages + p)], 0, 0))

    vec = pl.BlockSpec((None, 1, w), lambda b, j, pt: (b, 0, 0))
    const = lambda shape: pl.BlockSpec(shape, lambda b, j, pt: (0,) * len(shape))
    return pl.pallas_call(
        functools.partial(_sb_decode_body, pages=pages, past_len=past_len),
        grid_spec=pltpu.PrefetchScalarGridSpec(
            num_scalar_prefetch=1,
            grid=(nb, steps),
            in_specs=[vec, vec, vec, const(bias8.shape), const(tri.shape)]
            + [page_spec(p) for p in range(pages)] + [page_spec(p) for p in range(pages)],
            out_specs=vec,
            scratch_shapes=[pltpu.VMEM((H_SB, w), F32), pltpu.VMEM((H_SB, LANES), F32)],
        ),
        out_shape=jax.ShapeDtypeStruct((nb, 1, w), BF16),
        compiler_params=_params("parallel", "arbitrary"),
        name="sb_decode",
    )(page_table, q, k_new, v_new, bias8, tri, *([cache_kt] * pages), *([cache_vt] * pages))


def _gla_step_body(q_ref, k_ref, la_ref, v_ref, r_ref, gn_ref, s0_ref, o_ref, s1_ref):
    nb = q_ref.shape[0]
    rowid = lax.broadcasted_iota(jnp.int32, (8, WK_GLA), 0)
    own = rowid == lax.broadcasted_iota(jnp.int32, (8, WK_GLA), 1) // DK_GLA
    vrow = lax.broadcasted_iota(jnp.int32, (8, DV_GLA), 0)
    ones = jnp.ones((8, DV_GLA), BF16)
    gn = gn_ref[...]

    def rows_of_heads(x):
        out = jnp.broadcast_to(x[:, :DV_GLA], (8, DV_GLA))
        for h in range(1, H_GLA):
            out = jnp.where(vrow == h, jnp.broadcast_to(x[:, h * DV_GLA:(h + 1) * DV_GLA],
                                                        (8, DV_GLA)), out)
        return out

    for i in range(nb):
        q = q_ref[i:i + 1, :]
        k = k_ref[i:i + 1, :]
        decay = jnp.exp(la_ref[i:i + 1, :])
        s0 = s0_ref[i]
        v8 = rows_of_heads(v_ref[i:i + 1, :])
        r8 = rows_of_heads(r_ref[i:i + 1, :])
        spread = lambda x: jnp.where(own, jnp.broadcast_to(x, (8, WK_GLA)), 0.0)
        qk = jnp.sum(spread(q * k), axis=1, keepdims=True)
        o = qk * v8 + _dot(spread(q * decay).astype(BF16), s0.astype(BF16))
        o = _rms(o, gn) * (r8 * jax.nn.sigmoid(r8))
        for h in range(H_GLA):
            o_ref[i:i + 1, h * DV_GLA:(h + 1) * DV_GLA] = o[h:h + 1, :].astype(o_ref.dtype)
        d_hi = decay.astype(BF16).astype(F32)
        dd = jnp.where(rowid == 0, jnp.broadcast_to(d_hi, (8, WK_GLA)),
                       jnp.where(rowid == 1, jnp.broadcast_to(decay - d_hi, (8, WK_GLA)), 0.0))
        decay_col = _dot_tn(dd.astype(BF16), ones)
        s1_ref[i] = decay_col * s0 + _dot_tn(spread(k).astype(BF16), v8.astype(BF16))


def _gla_step(qg, kg, la, vg, rg, gn, s0, layer, nb_blk=8):
    nb = qg.shape[0]
    blk0 = layer * (nb // nb_blk)
    row = lambda n: pl.BlockSpec((nb_blk, n), lambda i: (i, 0))
    st = pl.BlockSpec((nb_blk, WK_GLA, DV_GLA), lambda i: (i, 0, 0))
    st_in = pl.BlockSpec((nb_blk, WK_GLA, DV_GLA), lambda i: (blk0 + i, 0, 0))
    o, s1 = pl.pallas_call(
        _gla_step_body,
        grid=(nb // nb_blk,),
        in_specs=[row(WK_GLA), row(WK_GLA), row(WK_GLA), row(WV_GLA), row(WV_GLA),
                  _full(gn.shape), st_in],
        out_specs=[row(WV_GLA), st],
        out_shape=[jax.ShapeDtypeStruct((nb, WV_GLA), BF16),
                   jax.ShapeDtypeStruct((nb, WK_GLA, DV_GLA), F32)],
        compiler_params=_params("parallel"),
        name="gla_step",
    )(qg, kg, la, vg, rg, gn, s0)
    return o, s1.reshape(nb, H_GLA, DK_GLA, DV_GLA)


def _prep_layer(l, ffn1_norm, ffn1_w_in, ffn1_w_out, mix_norm, w_in, b_gate, q_norm, k_norm,
                sb_bias, w_a2, b_a, gla_norm, w_o_sb, w_o_gla, w_out, ffn2_norm, ffn2_w_in,
                ffn2_w_out):
    d = w_in.shape[1]
    d_ff = ffn1_w_out.shape[1]
    bf = lambda a: a.astype(BF16)
    o_gl = 3 * W_SB
    o_alr = o_gl + 2 * WK_GLA + 2 * WV_GLA
    o_gt = o_alr + GATE_RANK
    win = w_in[l]
    head = jnp.arange(W_SB) // DH_SB
    return {
        "ffn1": (ffn1_norm[l][None], bf(ffn1_w_in[l][:, :d_ff]), bf(ffn1_w_in[l][:, d_ff:]),
                 bf(ffn1_w_out[l])),
        "ffn2": (ffn2_norm[l][None], bf(ffn2_w_in[l][:, :d_ff]), bf(ffn2_w_in[l][:, d_ff:]),
                 bf(ffn2_w_out[l])),
        "mix_norm": mix_norm[l][None],
        "w_sb": bf(win[:, :o_gl]),
        "w_gl": bf(win[:, o_gl:o_alr]),
        "w_alr": bf(jnp.pad(win[:, o_alr:o_gt], ((0, 0), (0, LANES - GATE_RANK)))),
        "w_a2": bf(jnp.pad(w_a2[l], ((0, LANES - GATE_RANK), (0, 0)))),
        "b_a": b_a[l][None],
        "w_gt": bf(win[:, o_gt:]),
        "b_gt": b_gate[l][None],
        "q_norm": jnp.tile(q_norm[l], H_SB)[None],
        "k_norm": jnp.tile(k_norm[l], H_SB)[None],
        "blk": (head[:, None] == head[None, :]).astype(BF16),
        "sb_bias": sb_bias[l],
        "gla_norm": gla_norm[l][None],
        "w_o_sb": bf(w_o_sb[l]),
        "w_o_gla": bf(w_o_gla[l]),
        "w_out": bf(w_out[l]),
    }


def _strict_upper(n):
    i = jnp.arange(n)
    return (i[:, None] > i[None, :]).astype(BF16)


def _prompt_layer(x, w):
    b, s, d = x.shape
    tm = 512
    x2 = x.reshape(b * s, d)
    x2 = _ffn(x2, *w["ffn1"], 2 * tm)
    q, kt, vt, kb, vb, qg, kg, la, vg, rg, gt = _inproj(x2, w, tm, seq=s)
    sh = lambda a: a.reshape(b, s, a.shape[-1])
    heads = lambda a: jnp.transpose(a.reshape(b, H_SB, DH_SB, s), (0, 3, 1, 2))
    osb = _sb_prompt(sh(q), sh(kb), sh(vb), w["sb_bias"], _strict_upper(SB_TILE))
    cs = min(GLA_CHUNK, s)
    i = jnp.arange(cs)
    tril = (i[None, :] <= i[:, None]).astype(BF16)
    og, st = _gla_prompt(sh(qg), sh(kg), sh(la), sh(vg), sh(rg), w["gla_norm"], tril)
    x2 = _merge(x2, osb.reshape(b * s, W_SB), og.reshape(b * s, WV_GLA), gt, w, tm)
    x2 = _ffn(x2, *w["ffn2"], 2 * tm)
    return (x2.reshape(b, s, d), heads(kt), heads(vt), st)


def _sample_layer(x, cache_k, cache_v, state, page_table, layer, w):
    b, s, d = x.shape
    assert s == 1
    tm = b
    x2 = x.reshape(b, d)
    x2 = _ffn(x2, *w["ffn1"], tm)
    q, k, v, _, _, qg, kg, la, vg, rg, gt = _inproj(x2, w, tm)
    depth, n_pool, page = cache_k.shape[:3]
    bias8 = jnp.broadcast_to(w["sb_bias"][:, None], (H_SB, LANES))
    pool_t = lambda c: jnp.transpose(c, (0, 1, 3, 4, 2)).reshape(depth * n_pool, W_SB, page)
    osb = _sb_decode(q.reshape(b, 1, W_SB), k.reshape(b, 1, W_SB), v.reshape(b, 1, W_SB),
                     pool_t(cache_k), pool_t(cache_v), layer * n_pool, page_table, bias8,
                     _strict_upper(page))
    og, st = _gla_step(qg, kg, la, vg, rg, w["gla_norm"],
                       state.reshape(depth * b, WK_GLA, DV_GLA), layer)
    x2 = _merge(x2, osb.reshape(b, W_SB), og, gt, w, tm)
    x2 = _ffn(x2, *w["ffn2"], tm)
    return (x2.reshape(b, s, d), k.reshape(b, s, H_SB, DH_SB), v.reshape(b, s, H_SB, DH_SB), st)


def kernel(x_prompt, x_sample, cache_k, cache_v, state_gla, page_table, ffn1_norm, ffn1_w_in,
           ffn1_w_out, mix_norm, w_in, b_gate, q_norm, k_norm, sb_bias, w_a2, b_a, gla_norm,
           w_o_sb, w_o_gla, w_out, ffn2_norm, ffn2_w_in, ffn2_w_out):
    depth = w_in.shape[0]
    yp, ys = x_prompt, x_sample
    outs = [[] for _ in range(6)]
    for l in range(depth):
        w = _prep_layer(l, ffn1_norm, ffn1_w_in, ffn1_w_out, mix_norm, w_in, b_gate, q_norm,
                        k_norm, sb_bias, w_a2, b_a, gla_norm, w_o_sb, w_o_gla, w_out, ffn2_norm,
                        ffn2_w_in, ffn2_w_out)
        yp, kp, vp, sp = _prompt_layer(yp, w)
        ys, kd, vd, sd = _sample_layer(ys, cache_k, cache_v, state_gla, page_table, l, w)
        for acc, val in zip(outs, (kp, vp, sp, kd, vd, sd)):
            acc.append(val)
    return (yp, ys) + tuple(jnp.stack(o) for o in outs)
```

```python
import functools

import jax
import jax.numpy as jnp
from jax import lax
from jax.experimental import pallas as pl
from jax.experimental.pallas import tpu as pltpu

F32 = jnp.float32
BF16 = jnp.bfloat16

EPS = 1e-6
H_SB = 8
DH_SB = 64
W_SB = H_SB * DH_SB
H_GLA = 4
DK_GLA = 64
DV_GLA = 128
WK_GLA = H_GLA * DK_GLA
WV_GLA = H_GLA * DV_GLA
GATE_RANK = 16
GATE_TAU = 16.0
GLA_CHUNK = 64

LANES = 128
VMEM_LIMIT_BYTES = 56 * 1024 * 1024
SB_TILE = 256
DEC_PAGES_PER_STEP = 16

_NT = (((1,), (1,)), ((), ()))
_TN = (((0,), (0,)), ((), ()))


def _dot(a, b):
    return jnp.dot(a, b, preferred_element_type=F32)


def _dot_nt(a, b):
    return lax.dot_general(a, b, _NT, preferred_element_type=F32)


def _dot_tn(a, b):
    return lax.dot_general(a, b, _TN, preferred_element_type=F32)


def _rms(x, g):
    return x * lax.rsqrt(jnp.mean(x * x, axis=-1, keepdims=True) + EPS) * g


def _neg_softplus(z):
    return -(jnp.maximum(z, 0.0) + jnp.log1p(jnp.exp(-jnp.abs(z))))


def _split_bf16(x):
    hi = x.astype(BF16)
    lo = (x - hi.astype(F32)).astype(BF16)
    return hi, lo


def _params(*sem):
    return pltpu.CompilerParams(dimension_semantics=sem, vmem_limit_bytes=VMEM_LIMIT_BYTES)


def _full(shape):
    nd = len(shape)
    return pl.BlockSpec(shape, lambda *_: (0,) * nd)


def _ffn_body(x_ref, g_ref, wa_ref, wb_ref, wo_ref, o_ref, *, chunks):
    x = x_ref[...]
    xn = _rms(x, g_ref[...]).astype(BF16)
    acc = None
    for c0, cw in chunks:
        a = _dot(xn, wa_ref[:, c0:c0 + cw])
        b = _dot(xn, wb_ref[:, c0:c0 + cw])
        h = (a * jax.nn.sigmoid(a) * b).astype(BF16)
        y = _dot(h, wo_ref[c0:c0 + cw, :])
        acc = y if acc is None else acc + y
    o_ref[...] = x + 0.5 * acc


def _ffn_chunks(f, width=1024):
    assert f % 256 == 0
    return tuple((c0, min(width, f - c0)) for c0 in range(0, f, width))


def _ffn(x, g, wa, wb, wo, tm):
    t, d = x.shape
    f = wa.shape[1]
    chunks = _ffn_chunks(f)
    row = pl.BlockSpec((tm, d), lambda i: (i, 0))
    return pl.pallas_call(
        functools.partial(_ffn_body, chunks=chunks),
        grid=(t // tm,),
        in_specs=[row, _full((1, d)), _full((d, f)), _full((d, f)), _full((f, d))],
        out_specs=row,
        out_shape=jax.ShapeDtypeStruct((t, d), F32),
        compiler_params=_params("parallel"),
        name="ffn",
    )(x, g, wa, wb, wo)


def _inproj_body(x_ref, g_ref, wsb_ref, wgl_ref, walr_ref, wa2_ref, ba_ref, wgt_ref, bgt_ref,
                 qn_ref, kn_ref, blk_ref,
                 q_ref, k_ref, v_ref, kb_ref, vb_ref, qg_ref, kg_ref, la_ref, vg_ref, rg_ref,
                 gt_ref, *, kv_transposed):
    h = _rms(x_ref[...], g_ref[...]).astype(BF16)

    def head_norm(t, gain):
        ss = _dot((t * t).astype(BF16), blk_ref[...])
        return t * lax.rsqrt(ss * (1.0 / DH_SB) + EPS) * gain

    zsb = _dot(h, wsb_ref[...])
    q = head_norm(zsb[:, :W_SB], qn_ref[...])
    k = head_norm(zsb[:, W_SB:2 * W_SB], kn_ref[...])
    v = zsb[:, 2 * W_SB:]
    q_ref[...] = (q * (DH_SB ** -0.5)).astype(BF16)
    k_ref[...] = k.T if kv_transposed else k
    v_ref[...] = v.T if kv_transposed else v
    kb_ref[...] = k.astype(BF16)
    vb_ref[...] = v.astype(BF16)

    zgl = _dot(h, wgl_ref[...])
    qg_ref[...] = zgl[:, :WK_GLA] * (DK_GLA ** -0.5)
    kg_ref[...] = zgl[:, WK_GLA:2 * WK_GLA]
    vg_ref[...] = zgl[:, 2 * WK_GLA:2 * WK_GLA + WV_GLA]
    rg_ref[...] = zgl[:, 2 * WK_GLA + WV_GLA:]

    alr = _dot(h, walr_ref[...])
    lin = _dot(alr.astype(BF16), wa2_ref[...]) + ba_ref[...]
    la_ref[...] = _neg_softplus(-lin) * (1.0 / GATE_TAU)

    gt_ref[...] = jax.nn.sigmoid(_dot(h, wgt_ref[...]) + bgt_ref[...]).astype(BF16)


def _inproj(x, w, tm, seq=None):
    t, d = x.shape
    row = lambda n: pl.BlockSpec((tm, n), lambda i: (i, 0))
    ins = [x, w["mix_norm"], w["w_sb"], w["w_gl"], w["w_alr"], w["w_a2"], w["b_a"], w["w_gt"],
           w["b_gt"], w["q_norm"], w["k_norm"], w["blk"]]
    in_specs = [row(d)] + [_full(a.shape) for a in ins[1:]]
    outs = [(W_SB, BF16), (W_SB, F32), (W_SB, F32), (W_SB, BF16), (W_SB, BF16),
            (WK_GLA, F32), (WK_GLA, F32), (WK_GLA, F32), (WV_GLA, F32), (WV_GLA, F32),
            (2 * d, BF16)]
    out_specs = [row(n) for n, _ in outs]
    out_shape = [jax.ShapeDtypeStruct((t, n), dt) for n, dt in outs]
    if seq is not None:
        per_seq = seq // tm
        kv_t = pl.BlockSpec((None, W_SB, tm), lambda i: (i // per_seq, 0, i % per_seq))
        out_specs[1] = out_specs[2] = kv_t
        out_shape[1] = out_shape[2] = jax.ShapeDtypeStruct((t // seq, W_SB, seq), F32)
    return pl.pallas_call(
        functools.partial(_inproj_body, kv_transposed=seq is not None),
        grid=(t // tm,),
        in_specs=in_specs,
        out_specs=out_specs,
        out_shape=out_shape,
        compiler_params=_params("parallel"),
        name="inproj",
    )(*ins)


def _sb_prompt_body(bias_ref, q_ref, k_ref, v_ref, tri_ref, o_ref):
    pair = pl.program_id(1)
    qi = pl.program_id(2)
    tq = q_ref.shape[0]
    tk = tri_ref.shape[0]
    assert tq == 2 * tk
    lane = lax.broadcasted_iota(jnp.int32, (1, LANES), 1)
    head_lo = lane < DH_SB
    q2 = q_ref[...]
    zero = jnp.zeros_like(q2)
    qa = jnp.where(head_lo, q2, zero)
    qb = jnp.where(head_lo, zero, q2)
    q_early = jnp.concatenate([qa[:tk], qb[:tk]], axis=0)
    q_late = jnp.concatenate([qa[tk:], qb[tk:]], axis=0)
    first = lax.broadcasted_iota(jnp.int32, (2 * tk, 1), 0) < tk
    bias2 = jnp.where(first, bias_ref[2 * pair], bias_ref[2 * pair + 1])
    tri = tri_ref[...]
    t_pos = lax.broadcasted_iota(jnp.int32, (2 * tk, tk), 0) & (tk - 1)
    s_pos = lax.broadcasted_iota(jnp.int32, (2 * tk, tk), 1)
    causal = s_pos < t_pos

    def block(qs, bias, start, carry, acc, mask):
        start = pl.multiple_of(start, tk)
        kblk = k_ref[pl.ds(start, tk), :]
        vblk = v_ref[pl.ds(start, tk), :]
        z = _dot_nt(qs, kblk) + bias
        sp = jnp.maximum(z, 0.0) + jnp.log(1.0 + jnp.exp(-jnp.abs(z)))
        if mask is not None:
            sp = jnp.where(mask, sp, 0.0)
        later = _dot(sp.astype(BF16), tri)
        att = jnp.exp(z - sp - later - carry)
        if mask is not None:
            att = jnp.where(mask, att, 0.0)
        acc = acc + _dot(att.astype(BF16), vblk)
        carry = carry + jnp.sum(sp, axis=1, keepdims=True)
        return carry, acc

    q0 = qi * tq
    zc = jnp.zeros((2 * tk, 1), F32)
    za = jnp.zeros((2 * tk, LANES), F32)
    c_late, a_late = block(q_late, bias2, q0 + tk, zc, za, causal)
    c_late, a_late = block(q_late, bias2, q0, c_late, a_late, None)
    c_early, a_early = block(q_early, bias2, q0, zc, za, causal)

    qs = jnp.concatenate([q_early, q_late], axis=0)
    bias4 = jnp.concatenate([bias2, bias2], axis=0)
    carry = jnp.concatenate([c_early, c_late], axis=0)
    acc = jnp.concatenate([a_early, a_late], axis=0)

    def two_tiles(i, c):
        c = block(qs, bias4, q0 - (2 * i + 1) * tk, c[0], c[1], None)
        return block(qs, bias4, q0 - (2 * i + 2) * tk, c[0], c[1], None)

    carry, acc = lax.fori_loop(0, qi, two_tiles, (carry, acc))
    o_ref[:tk, :] = jnp.where(head_lo, acc[:tk], acc[tk:2 * tk]).astype(o_ref.dtype)
    o_ref[tk:, :] = jnp.where(head_lo, acc[2 * tk:3 * tk], acc[3 * tk:]).astype(o_ref.dtype)


def _sb_prompt(q, k, v, bias, tri):
    b, s, w = q.shape
    tq = 2 * tri.shape[0]
    qspec = pl.BlockSpec((None, tq, LANES), lambda bi, p, i, *_: (bi, i, p))
    kvspec = pl.BlockSpec((None, s, LANES), lambda bi, p, i, *_: (bi, 0, p))
    return pl.pallas_call(
        _sb_prompt_body,
        grid_spec=pltpu.PrefetchScalarGridSpec(
            num_scalar_prefetch=1,
            grid=(b, w // LANES, s // tq),
            in_specs=[qspec, kvspec, kvspec, pl.BlockSpec(tri.shape, lambda *_: (0, 0))],
            out_specs=qspec,
        ),
        out_shape=jax.ShapeDtypeStruct((b, s, w), BF16),
        compiler_params=_params("parallel", "parallel", "arbitrary"),
        name="sb_prompt",
    )(bias, q, k, v, tri)


def _gla_chunk(q_ref, k_ref, la_ref, v_ref, r_ref, gn, tril, o_ref, stt_ref):
    cs = q_ref.shape[0]
    lane = lax.broadcasted_iota(jnp.int32, (1, LANES), 1)
    head_lo = lane < DK_GLA
    ti = lax.broadcasted_iota(jnp.int32, (cs, cs), 0)
    tj = lax.broadcasted_iota(jnp.int32, (cs, cs), 1)
    causal = tj <= ti
    la_hi, la_lo = _split_bf16(la_ref[...])
    bc = _dot(tril, la_hi) + _dot(tril, la_lo)
    b_last = bc[cs - 1:cs, :]
    q_dec = q_ref[...] * jnp.exp(bc)
    k_inv = (k_ref[...] * jnp.exp(-bc)).astype(BF16)
    k_end = (k_ref[...] * jnp.exp(b_last - bc)).astype(BF16)
    decay = jnp.exp(b_last)
    vb = v_ref[...].astype(BF16)
    rb = r_ref[...]
    for p in range(H_GLA // 2):
        ps = slice(p * LANES, (p + 1) * LANES)
        qd2 = q_dec[:, ps]
        ki2 = k_inv[:, ps]
        ke2 = k_end[:, ps]
        stt = stt_ref[p]
        stt_b = stt.astype(BF16)
        kvs = []
        for hh in range(2):
            h = 2 * p + hh
            hs = slice(h * DV_GLA, (h + 1) * DV_GLA)
            in_head = head_lo if hh == 0 else jnp.logical_not(head_lo)
            qa = jnp.where(in_head, qd2, 0.0).astype(BF16)
            att = jnp.where(causal, _dot_nt(qa, ki2), 0.0)
            vh = vb[:, hs]
            o = _dot(att.astype(BF16), vh) + _dot_nt(qa, stt_b)
            o = _rms(o, gn)
            r = rb[:, hs]
            o_ref[:, hs] = (o * (r * jax.nn.sigmoid(r))).astype(o_ref.dtype)
            kvs.append(_dot_tn(vh, ke2))
        stt_ref[p] = decay[:, ps] * stt + jnp.where(head_lo, kvs[0], kvs[1])


def _own_lanes(w):
    rowid = lax.broadcasted_iota(jnp.int32, (H_SB, w), 0)
    return rowid == lax.broadcasted_iota(jnp.int32, (H_SB, w), 1) // DH_SB


def _spread_query(q_ref):
    w = q_ref.shape[1]
    return jnp.where(_own_lanes(w), jnp.broadcast_to(q_ref[...].astype(F32), (H_SB, w)), 0.0)


def _decode_new_token(q_ref, kn_ref, vn_ref, bias, acc_ref, carry_ref, past_len):
    w = q_ref.shape[1]
    s_pos = past_len + lax.broadcasted_iota(jnp.int32, (H_SB, 1), 1)
    visible = s_pos < past_len
    z = jnp.sum(_spread_query(q_ref) * kn_ref[...], axis=1, keepdims=True) + bias[:, :1]
    l1m = jnp.where(visible, _neg_softplus(z), 0.0)
    att = jnp.where(visible, jnp.exp(z + l1m), 0.0)
    acc_ref[...] = att * jnp.broadcast_to(vn_ref[...], (H_SB, w))
    carry_ref[...] = jnp.broadcast_to(l1m, carry_ref.shape)


def _decode_pages(q_ref, bias, tri, k_refs, v_refs, acc_ref, carry_ref):
    pages = len(k_refs)
    qb = _spread_query(q_ref).astype(BF16)
    z3 = jnp.stack([_dot(qb, k_refs[p][...].astype(BF16)) for p in range(pages)]) + bias[None]
    l1m = _neg_softplus(z3)
    hi, lo = _split_bf16(l1m.reshape(pages * H_SB, LANES))
    within = (_dot(hi, tri) + _dot(lo, tri)).reshape(pages, H_SB, LANES)
    page_sum = jnp.sum(l1m, axis=2, keepdims=True)
    carry = carry_ref[:, :1]
    carries = []
    for p in range(pages):
        carries.append(carry)
        carry = carry + page_sum[p]
    att = jnp.exp(z3 + l1m + within + jnp.stack(carries)).astype(BF16)
    acc = acc_ref[...]
    for p in range(pages):
        acc = acc + _dot_nt(att[p], v_refs[p][...].astype(BF16))
    acc_ref[...] = acc
    carry_ref[...] = jnp.broadcast_to(carry, carry_ref.shape)


def _gla_decode_body(pt_ref, gq_ref, gk_ref, gla_ref, gv_ref, gr_ref, gn_ref, tril_ref,
                     q_ref, kn_ref, vn_ref, bias_ref, tri_ref, *rest,
                     pages, past_len, n_seq, n_chunks, dec_steps):
    k_refs = rest[:pages]
    v_refs = rest[pages:2 * pages]
    og_ref, st_ref, osb_ref, stt_ref, acc_ref, carry_ref = rest[2 * pages:]
    i = pl.program_id(0)
    seq = i % n_seq
    c = i // n_seq
    j = i % dec_steps
    bias = bias_ref[...]

    @pl.when(c == 0)
    def _():
        stt_ref[seq] = jnp.zeros(stt_ref.shape[1:], F32)

    @pl.when(j == 0)
    def _():
        _decode_new_token(q_ref, kn_ref, vn_ref, bias, acc_ref, carry_ref, past_len)

    _gla_chunk(gq_ref, gk_ref, gla_ref, gv_ref, gr_ref, gn_ref[...], tril_ref[...], og_ref,
               stt_ref.at[seq])
    _decode_pages(q_ref, bias, tri_ref[...], k_refs, v_refs, acc_ref, carry_ref)

    @pl.when(c == n_chunks - 1)
    def _():
        for p in range(H_GLA // 2):
            st_ref[seq, p] = stt_ref[seq, p].T

    @pl.when(j == dec_steps - 1)
    def _():
        own = _own_lanes(acc_ref.shape[1])
        osb_ref[...] = jnp.sum(jnp.where(own, acc_ref[...], 0.0), axis=0,
                               keepdims=True).astype(osb_ref.dtype)


def _gla_decode(qg, kg, la, vg, rg, gn, tril,
                q, k_new, v_new, cache_kt, cache_vt, pool_offset, page_table, bias8, tri):
    n_seq, s, _ = qg.shape
    cs = min(GLA_CHUNK, s)
    n_chunks = s // cs
    nb, n_pages = page_table.shape
    _, w, page = cache_kt.shape
    pages = DEC_PAGES_PER_STEP
    dec_steps = n_pages // pages
    past_len = n_pages * page
    n_steps = n_seq * n_chunks
    assert s % cs == 0 and n_pages % pages == 0 and page == LANES
    assert n_steps == nb * dec_steps, "both jobs must have the same number of grid steps"

    chunk = lambda n: pl.BlockSpec((None, cs, n), lambda i, pt: (i % n_seq, i // n_seq, 0))
    const = lambda shape: pl.BlockSpec(shape, lambda i, pt: (0,) * len(shape))
    vec = pl.BlockSpec((None, 1, w), lambda i, pt: (i // dec_steps, 0, 0))

    def page_spec(p):
        def index(i, pt):
            j = i % dec_steps
            return (pool_offset + pt[i // dec_steps, n_pages - 1 - (j * pages + p)], 0, 0)
        return pl.BlockSpec((None, w, page), index)

    st_shape = (n_seq, H_GLA // 2, 2 * DK_GLA, DV_GLA)
    og, st, osb = pl.pallas_call(
        functools.partial(_gla_decode_body, pages=pages, past_len=past_len, n_seq=n_seq,
                          n_chunks=n_chunks, dec_steps=dec_steps),
        grid_spec=pltpu.PrefetchScalarGridSpec(
            num_scalar_prefetch=1,
            grid=(n_steps,),
            in_specs=[chunk(WK_GLA), chunk(WK_GLA), chunk(WK_GLA), chunk(WV_GLA), chunk(WV_GLA),
                      const(gn.shape), const(tril.shape),
                      vec, vec, vec, const(bias8.shape), const(tri.shape)]
            + [page_spec(p) for p in range(pages)] + [page_spec(p) for p in range(pages)],
            out_specs=[chunk(WV_GLA), const(st_shape), vec],
            scratch_shapes=[pltpu.VMEM((n_seq, H_GLA // 2, DV_GLA, 2 * DK_GLA), F32),
                            pltpu.VMEM((H_SB, w), F32), pltpu.VMEM((H_SB, LANES), F32)],
        ),
        out_shape=[jax.ShapeDtypeStruct((n_seq, s, WV_GLA), BF16),
                   jax.ShapeDtypeStruct(st_shape, F32),
                   jax.ShapeDtypeStruct((nb, 1, w), BF16)],
        compiler_params=_params("arbitrary"),
        name="gla_decode",
    )(page_table, qg, kg, la, vg, rg, gn, tril, q, k_new, v_new, bias8, tri,
      *([cache_kt] * pages), *([cache_vt] * pages))
    return og, st.reshape(n_seq, H_GLA, DK_GLA, DV_GLA), osb


def _merge_body(x_ref, osb_ref, og_ref, gt_ref, wosb_ref, wogl_ref, wout_ref, o_ref):
    d = x_ref.shape[1]
    g = gt_ref[...].astype(F32)
    m = g[:, :d] * _dot(osb_ref[...], wosb_ref[...]) + g[:, d:] * _dot(og_ref[...], wogl_ref[...])
    o_ref[...] = x_ref[...] + _dot(m.astype(BF16), wout_ref[...])


def _merge(x, osb, og, gt, w, tm):
    t, d = x.shape
    row = lambda n: pl.BlockSpec((tm, n), lambda i: (i, 0))
    return pl.pallas_call(
        _merge_body,
        grid=(t // tm,),
        in_specs=[row(d), row(W_SB), row(WV_GLA), row(2 * d),
                  _full(w["w_o_sb"].shape), _full(w["w_o_gla"].shape), _full(w["w_out"].shape)],
        out_specs=row(d),
        out_shape=jax.ShapeDtypeStruct((t, d), F32),
        compiler_params=_params("parallel"),
        name="merge",
    )(x, osb, og, gt, w["w_o_sb"], w["w_o_gla"], w["w_out"])


def _gla_step_body(q_ref, k_ref, la_ref, v_ref, r_ref, gn_ref, s0_ref, o_ref, s1_ref):
    nb = q_ref.shape[0]
    rowid = lax.broadcasted_iota(jnp.int32, (8, WK_GLA), 0)
    own = rowid == lax.broadcasted_iota(jnp.int32, (8, WK_GLA), 1) // DK_GLA
    vrow = lax.broadcasted_iota(jnp.int32, (8, DV_GLA), 0)
    ones = jnp.ones((8, DV_GLA), BF16)
    gn = gn_ref[...]

    def rows_of_heads(x):
        out = jnp.broadcast_to(x[:, :DV_GLA], (8, DV_GLA))
        for h in range(1, H_GLA):
            out = jnp.where(vrow == h, jnp.broadcast_to(x[:, h * DV_GLA:(h + 1) * DV_GLA],
                                                        (8, DV_GLA)), out)
        return out

    for i in range(nb):
        q = q_ref[i:i + 1, :]
        k = k_ref[i:i + 1, :]
        decay = jnp.exp(la_ref[i:i + 1, :])
        s0 = s0_ref[i]
        v8 = rows_of_heads(v_ref[i:i + 1, :])
        r8 = rows_of_heads(r_ref[i:i + 1, :])
        spread = lambda x: jnp.where(own, jnp.broadcast_to(x, (8, WK_GLA)), 0.0)
        qk = jnp.sum(spread(q * k), axis=1, keepdims=True)
        o = qk * v8 + _dot(spread(q * decay).astype(BF16), s0.astype(BF16))
        o = _rms(o, gn) * (r8 * jax.nn.sigmoid(r8))
        for h in range(H_GLA):
            o_ref[i:i + 1, h * DV_GLA:(h + 1) * DV_GLA] = o[h:h + 1, :].astype(o_ref.dtype)
        d_hi = decay.astype(BF16).astype(F32)
        dd = jnp.where(rowid == 0, jnp.broadcast_to(d_hi, (8, WK_GLA)),
                       jnp.where(rowid == 1, jnp.broadcast_to(decay - d_hi, (8, WK_GLA)), 0.0))
        decay_col = _dot_tn(dd.astype(BF16), ones)
        s1_ref[i] = decay_col * s0 + _dot_tn(spread(k).astype(BF16), v8.astype(BF16))


def _gla_step(qg, kg, la, vg, rg, gn, s0, layer, nb_blk=8):
    nb = qg.shape[0]
    blk0 = layer * (nb // nb_blk)
    row = lambda n: pl.BlockSpec((nb_blk, n), lambda i: (i, 0))
    st = pl.BlockSpec((nb_blk, WK_GLA, DV_GLA), lambda i: (i, 0, 0))
    st_in = pl.BlockSpec((nb_blk, WK_GLA, DV_GLA), lambda i: (blk0 + i, 0, 0))
    o, s1 = pl.pallas_call(
        _gla_step_body,
        grid=(nb // nb_blk,),
        in_specs=[row(WK_GLA), row(WK_GLA), row(WK_GLA), row(WV_GLA), row(WV_GLA),
                  _full(gn.shape), st_in],
        out_specs=[row(WV_GLA), st],
        out_shape=[jax.ShapeDtypeStruct((nb, WV_GLA), BF16),
                   jax.ShapeDtypeStruct((nb, WK_GLA, DV_GLA), F32)],
        compiler_params=_params("parallel"),
        name="gla_step",
    )(qg, kg, la, vg, rg, gn, s0)
    return o, s1.reshape(nb, H_GLA, DK_GLA, DV_GLA)


def _prep_layer(l, ffn1_norm, ffn1_w_in, ffn1_w_out, mix_norm, w_in, b_gate, q_norm, k_norm,
                sb_bias, w_a2, b_a, gla_norm, w_o_sb, w_o_gla, w_out, ffn2_norm, ffn2_w_in,
                ffn2_w_out):
    d = w_in.shape[1]
    d_ff = ffn1_w_out.shape[1]
    bf = lambda a: a.astype(BF16)
    o_gl = 3 * W_SB
    o_alr = o_gl + 2 * WK_GLA + 2 * WV_GLA
    o_gt = o_alr + GATE_RANK
    win = w_in[l]
    head = jnp.arange(W_SB) // DH_SB
    return {
        "ffn1": (ffn1_norm[l][None], bf(ffn1_w_in[l][:, :d_ff]), bf(ffn1_w_in[l][:, d_ff:]),
                 bf(ffn1_w_out[l])),
        "ffn2": (ffn2_norm[l][None], bf(ffn2_w_in[l][:, :d_ff]), bf(ffn2_w_in[l][:, d_ff:]),
                 bf(ffn2_w_out[l])),
        "mix_norm": mix_norm[l][None],
        "w_sb": bf(win[:, :o_gl]),
        "w_gl": bf(win[:, o_gl:o_alr]),
        "w_alr": bf(jnp.pad(win[:, o_alr:o_gt], ((0, 0), (0, LANES - GATE_RANK)))),
        "w_a2": bf(jnp.pad(w_a2[l], ((0, LANES - GATE_RANK), (0, 0)))),
        "b_a": b_a[l][None],
        "w_gt": bf(win[:, o_gt:]),
        "b_gt": b_gate[l][None],
        "q_norm": jnp.tile(q_norm[l], H_SB)[None],
        "k_norm": jnp.tile(k_norm[l], H_SB)[None],
        "blk": (head[:, None] == head[None, :]).astype(BF16),
        "sb_bias": sb_bias[l],
        "gla_norm": gla_norm[l][None],
        "w_o_sb": bf(w_o_sb[l]),
        "w_o_gla": bf(w_o_gla[l]),
        "w_out": bf(w_out[l]),
    }


def _strict_upper(n):
    i = jnp.arange(n)
    return (i[:, None] > i[None, :]).astype(BF16)


def _layer(xp, xs, cache_k, cache_v, state, page_table, layer, w):
    b, s, d = xp.shape
    nb, s1, _ = xs.shape
    assert s1 == 1
    tm = 512
    depth, n_pool, page = cache_k.shape[:3]

    x2 = _ffn(xp.reshape(b * s, d), *w["ffn1"], 2 * tm)
    q, kt, vt, kb, vb, qg, kg, la, vg, rg, gt = _inproj(x2, w, tm, seq=s)
    y2 = _ffn(xs.reshape(nb, d), *w["ffn1"], nb)
    qd, kd, vd, _, _, qgd, kgd, lad, vgd, rgd, gtd = _inproj(y2, w, nb)

    sh = lambda a: a.reshape(b, s, a.shape[-1])
    osb = _sb_prompt(sh(q), sh(kb), sh(vb), w["sb_bias"], _strict_upper(SB_TILE))

    cs = min(GLA_CHUNK, s)
    i = jnp.arange(cs)
    tril = (i[None, :] <= i[:, None]).astype(BF16)
    bias8 = jnp.broadcast_to(w["sb_bias"][:, None], (H_SB, LANES))
    pool_t = lambda c: jnp.transpose(c, (0, 1, 3, 4, 2)).reshape(depth * n_pool, W_SB, page)
    vec = lambda a: a.reshape(nb, 1, W_SB)
    og, stp, osd = _gla_decode(sh(qg), sh(kg), sh(la), sh(vg), sh(rg), w["gla_norm"], tril,
                               vec(qd), vec(kd), vec(vd), pool_t(cache_k), pool_t(cache_v),
                               layer * n_pool, page_table, bias8, _strict_upper(page))
    ogd, std = _gla_step(qgd, kgd, lad, vgd, rgd, w["gla_norm"],
                         state.reshape(depth * nb, WK_GLA, DV_GLA), layer)

    x2 = _merge(x2, osb.reshape(b * s, W_SB), og.reshape(b * s, WV_GLA), gt, w, tm)
    x2 = _ffn(x2, *w["ffn2"], 2 * tm)
    y2 = _merge(y2, osd.reshape(nb, W_SB), ogd, gtd, w, nb)
    y2 = _ffn(y2, *w["ffn2"], nb)

    heads = lambda a: jnp.transpose(a.reshape(b, H_SB, DH_SB, s), (0, 3, 1, 2))
    new = lambda a: a.reshape(nb, 1, H_SB, DH_SB)
    return (x2.reshape(b, s, d), y2.reshape(nb, 1, d),
            (heads(kt), heads(vt), stp, new(kd), new(vd), std))


def kernel(x_prompt, x_sample, cache_k, cache_v, state_gla, page_table, ffn1_norm, ffn1_w_in,
           ffn1_w_out, mix_norm, w_in, b_gate, q_norm, k_norm, sb_bias, w_a2, b_a, gla_norm,
           w_o_sb, w_o_gla, w_out, ffn2_norm, ffn2_w_in, ffn2_w_out):
    depth = w_in.shape[0]
    yp, ys = x_prompt, x_sample
    outs = [[] for _ in range(6)]
    for l in range(depth):
        w = _prep_layer(l, ffn1_norm, ffn1_w_in, ffn1_w_out, mix_norm, w_in, b_gate, q_norm,
                        k_norm, sb_bias, w_a2, b_a, gla_norm, w_o_sb, w_o_gla, w_out, ffn2_norm,
                        ffn2_w_in, ffn2_w_out)
        yp, ys, vals = _layer(yp, ys, cache_k, cache_v, state_gla, page_table, l, w)
        for acc, val in zip(outs, vals):
            acc.append(val)
    return (yp, ys) + tuple(jnp.stack(o) for o in outs)
```

```python
import functools

import jax
import jax.numpy as jnp
from jax import lax
from jax.experimental import pallas as pl
from jax.experimental.pallas import tpu as pltpu

F32 = jnp.float32
BF16 = jnp.bfloat16

EPS = 1e-6
H_SB = 8
DH_SB = 64
W_SB = H_SB * DH_SB
H_GLA = 4
DK_GLA = 64
DV_GLA = 128
WK_GLA = H_GLA * DK_GLA
WV_GLA = H_GLA * DV_GLA
GATE_RANK = 16
GATE_TAU = 16.0
GLA_CHUNK = 64

LANES = 128
VMEM_LIMIT_BYTES = 56 * 1024 * 1024
SB_TILE = 256
DEC_PAGES_PER_STEP = 16

_NT = (((1,), (1,)), ((), ()))
_TN = (((0,), (0,)), ((), ()))


def _dot(a, b):
    return jnp.dot(a, b, preferred_element_type=F32)


def _dot_nt(a, b):
    return lax.dot_general(a, b, _NT, preferred_element_type=F32)


def _dot_tn(a, b):
    return lax.dot_general(a, b, _TN, preferred_element_type=F32)


def _rms(x, g):
    return x * lax.rsqrt(jnp.mean(x * x, axis=-1, keepdims=True) + EPS) * g


def _neg_softplus(z):
    return -(jnp.maximum(z, 0.0) + jnp.log1p(jnp.exp(-jnp.abs(z))))


def _split_bf16(x):
    hi = x.astype(BF16)
    lo = (x - hi.astype(F32)).astype(BF16)
    return hi, lo


def _params(*sem):
    return pltpu.CompilerParams(dimension_semantics=sem, vmem_limit_bytes=VMEM_LIMIT_BYTES)


def _full(shape):
    nd = len(shape)
    return pl.BlockSpec(shape, lambda *_: (0,) * nd)


def _ffn_body(x_ref, g_ref, wa_ref, wb_ref, wo_ref, o_ref, *, chunks):
    x = x_ref[...]
    xn = _rms(x, g_ref[...]).astype(BF16)
    acc = None
    for c0, cw in chunks:
        a = _dot(xn, wa_ref[:, c0:c0 + cw])
        b = _dot(xn, wb_ref[:, c0:c0 + cw])
        h = (a * jax.nn.sigmoid(a) * b).astype(BF16)
        y = _dot(h, wo_ref[c0:c0 + cw, :])
        acc = y if acc is None else acc + y
    o_ref[...] = x + 0.5 * acc


def _ffn_chunks(f, width=1024):
    assert f % 256 == 0
    return tuple((c0, min(width, f - c0)) for c0 in range(0, f, width))


def _ffn(x, g, wa, wb, wo, tm):
    t, d = x.shape
    f = wa.shape[1]
    chunks = _ffn_chunks(f)
    row = pl.BlockSpec((tm, d), lambda i: (i, 0))
    return pl.pallas_call(
        functools.partial(_ffn_body, chunks=chunks),
        grid=(t // tm,),
        in_specs=[row, _full((1, d)), _full((d, f)), _full((d, f)), _full((f, d))],
        out_specs=row,
        out_shape=jax.ShapeDtypeStruct((t, d), F32),
        compiler_params=_params("parallel"),
        name="ffn",
    )(x, g, wa, wb, wo)


def _inproj_body(x_ref, g_ref, wsb_ref, wgl_ref, walr_ref, wa2_ref, ba_ref, wgt_ref, bgt_ref,
                 qn_ref, kn_ref, blk_ref,
                 q_ref, k_ref, v_ref, kb_ref, vb_ref, qg_ref, kg_ref, la_ref, vg_ref, rg_ref,
                 gt_ref, *, kv_transposed):
    h = _rms(x_ref[...], g_ref[...]).astype(BF16)

    def head_norm(t, gain):
        ss = _dot((t * t).astype(BF16), blk_ref[...])
        return t * lax.rsqrt(ss * (1.0 / DH_SB) + EPS) * gain

    zsb = _dot(h, wsb_ref[...])
    q = head_norm(zsb[:, :W_SB], qn_ref[...])
    k = head_norm(zsb[:, W_SB:2 * W_SB], kn_ref[...])
    v = zsb[:, 2 * W_SB:]
    q_ref[...] = (q * (DH_SB ** -0.5)).astype(BF16)
    k_ref[...] = k.T if kv_transposed else k
    v_ref[...] = v.T if kv_transposed else v
    kb_ref[...] = k.astype(BF16)
    vb_ref[...] = v.astype(BF16)

    zgl = _dot(h, wgl_ref[...])
    qg_ref[...] = zgl[:, :WK_GLA] * (DK_GLA ** -0.5)
    kg_ref[...] = zgl[:, WK_GLA:2 * WK_GLA]
    vg_ref[...] = zgl[:, 2 * WK_GLA:2 * WK_GLA + WV_GLA]
    rg_ref[...] = zgl[:, 2 * WK_GLA + WV_GLA:]

    alr = _dot(h, walr_ref[...])
    lin = _dot(alr.astype(BF16), wa2_ref[...]) + ba_ref[...]
    la_ref[...] = _neg_softplus(-lin) * (1.0 / GATE_TAU)

    gt_ref[...] = jax.nn.sigmoid(_dot(h, wgt_ref[...]) + bgt_ref[...]).astype(BF16)


def _inproj(x, w, tm, seq=None):
    t, d = x.shape
    row = lambda n: pl.BlockSpec((tm, n), lambda i: (i, 0))
    ins = [x, w["mix_norm"], w["w_sb"], w["w_gl"], w["w_alr"], w["w_a2"], w["b_a"], w["w_gt"],
           w["b_gt"], w["q_norm"], w["k_norm"], w["blk"]]
    in_specs = [row(d)] + [_full(a.shape) for a in ins[1:]]
    outs = [(W_SB, BF16), (W_SB, F32), (W_SB, F32), (W_SB, BF16), (W_SB, BF16),
            (WK_GLA, F32), (WK_GLA, F32), (WK_GLA, F32), (WV_GLA, F32), (WV_GLA, F32),
            (2 * d, BF16)]
    out_specs = [row(n) for n, _ in outs]
    out_shape = [jax.ShapeDtypeStruct((t, n), dt) for n, dt in outs]
    if seq is not None:
        per_seq = seq // tm
        kv_t = pl.BlockSpec((None, W_SB, tm), lambda i: (i // per_seq, 0, i % per_seq))
        out_specs[1] = out_specs[2] = kv_t
        out_shape[1] = out_shape[2] = jax.ShapeDtypeStruct((t // seq, W_SB, seq), F32)
    return pl.pallas_call(
        functools.partial(_inproj_body, kv_transposed=seq is not None),
        grid=(t // tm,),
        in_specs=in_specs,
        out_specs=out_specs,
        out_shape=out_shape,
        compiler_params=_params("parallel"),
        name="inproj",
    )(*ins)


def _sb_prompt_body(bias_ref, q_ref, k_ref, v_ref, tri_ref, o_ref):
    pair = pl.program_id(1)
    qi = pl.program_id(2)
    tq = q_ref.shape[0]
    tk = tri_ref.shape[0]
    assert tq == 2 * tk
    lane = lax.broadcasted_iota(jnp.int32, (1, LANES), 1)
    head_lo = lane < DH_SB
    q2 = q_ref[...]
    zero = jnp.zeros_like(q2)
    qa = jnp.where(head_lo, q2, zero)
    qb = jnp.where(head_lo, zero, q2)
    q_early = jnp.concatenate([qa[:tk], qb[:tk]], axis=0)
    q_late = jnp.concatenate([qa[tk:], qb[tk:]], axis=0)
    first = lax.broadcasted_iota(jnp.int32, (2 * tk, 1), 0) < tk
    bias2 = jnp.where(first, bias_ref[2 * pair], bias_ref[2 * pair + 1])
    tri = tri_ref[...]
    t_pos = lax.broadcasted_iota(jnp.int32, (2 * tk, tk), 0) & (tk - 1)
    s_pos = lax.broadcasted_iota(jnp.int32, (2 * tk, tk), 1)
    causal = s_pos < t_pos

    def block(qs, bias, start, carry, acc, mask):
        start = pl.multiple_of(start, tk)
        kblk = k_ref[pl.ds(start, tk), :]
        vblk = v_ref[pl.ds(start, tk), :]
        z = _dot_nt(qs, kblk) + bias
        sp = jnp.maximum(z, 0.0) + jnp.log(1.0 + jnp.exp(-jnp.abs(z)))
        if mask is not None:
            sp = jnp.where(mask, sp, 0.0)
        later = _dot(sp.astype(BF16), tri)
        att = jnp.exp(z - sp - later - carry)
        if mask is not None:
            att = jnp.where(mask, att, 0.0)
        acc = acc + _dot(att.astype(BF16), vblk)
        carry = carry + jnp.sum(sp, axis=1, keepdims=True)
        return carry, acc

    q0 = qi * tq
    zc = jnp.zeros((2 * tk, 1), F32)
    za = jnp.zeros((2 * tk, LANES), F32)
    c_late, a_late = block(q_late, bias2, q0 + tk, zc, za, causal)
    c_late, a_late = block(q_late, bias2, q0, c_late, a_late, None)
    c_early, a_early = block(q_early, bias2, q0, zc, za, causal)

    def two_tiles(i, c):
        c0, a0, c1, a1 = c
        for t in (1, 2):
            start = q0 - (2 * i + t) * tk
            c0, a0 = block(q_early, bias2, start, c0, a0, None)
            c1, a1 = block(q_late, bias2, start, c1, a1, None)
        return c0, a0, c1, a1

    _, a_early, _, a_late = lax.fori_loop(0, qi, two_tiles, (c_early, a_early, c_late, a_late))
    acc = jnp.concatenate([a_early, a_late], axis=0)
    o_ref[:tk, :] = jnp.where(head_lo, acc[:tk], acc[tk:2 * tk]).astype(o_ref.dtype)
    o_ref[tk:, :] = jnp.where(head_lo, acc[2 * tk:3 * tk], acc[3 * tk:]).astype(o_ref.dtype)


def _sb_prompt(q, k, v, bias, tri):
    b, s, w = q.shape
    tq = 2 * tri.shape[0]
    qspec = pl.BlockSpec((None, tq, LANES), lambda bi, p, i, *_: (bi, i, p))
    kvspec = pl.BlockSpec((None, s, LANES), lambda bi, p, i, *_: (bi, 0, p))
    return pl.pallas_call(
        _sb_prompt_body,
        grid_spec=pltpu.PrefetchScalarGridSpec(
            num_scalar_prefetch=1,
            grid=(b, w // LANES, s // tq),
            in_specs=[qspec, kvspec, kvspec, pl.BlockSpec(tri.shape, lambda *_: (0, 0))],
            out_specs=qspec,
        ),
        out_shape=jax.ShapeDtypeStruct((b, s, w), BF16),
        compiler_params=_params("parallel", "parallel", "arbitrary"),
        name="sb_prompt",
    )(bias, q, k, v, tri)


def _gla_chunk(q_ref, k_ref, la_ref, v_ref, r_ref, gn, tril, o_ref, stt_ref):
    cs = q_ref.shape[0]
    lane = lax.broadcasted_iota(jnp.int32, (1, LANES), 1)
    head_lo = lane < DK_GLA
    ti = lax.broadcasted_iota(jnp.int32, (cs, cs), 0)
    tj = lax.broadcasted_iota(jnp.int32, (cs, cs), 1)
    causal = tj <= ti
    la_hi, la_lo = _split_bf16(la_ref[...])
    bc = _dot(tril, la_hi) + _dot(tril, la_lo)
    b_last = bc[cs - 1:cs, :]
    q_dec = q_ref[...] * jnp.exp(bc)
    k_inv = (k_ref[...] * jnp.exp(-bc)).astype(BF16)
    k_end = (k_ref[...] * jnp.exp(b_last - bc)).astype(BF16)
    decay = jnp.exp(b_last)
    vb = v_ref[...].astype(BF16)
    rb = r_ref[...]
    for p in range(H_GLA // 2):
        ps = slice(p * LANES, (p + 1) * LANES)
        qd2 = q_dec[:, ps]
        ki2 = k_inv[:, ps]
        ke2 = k_end[:, ps]
        stt = stt_ref[p]
        stt_b = stt.astype(BF16)
        kvs = []
        for hh in range(2):
            h = 2 * p + hh
            hs = slice(h * DV_GLA, (h + 1) * DV_GLA)
            in_head = head_lo if hh == 0 else jnp.logical_not(head_lo)
            qa = jnp.where(in_head, qd2, 0.0).astype(BF16)
            att = jnp.where(causal, _dot_nt(qa, ki2), 0.0)
            vh = vb[:, hs]
            o = _dot(att.astype(BF16), vh) + _dot_nt(qa, stt_b)
            o = _rms(o, gn)
            r = rb[:, hs]
            o_ref[:, hs] = (o * (r * jax.nn.sigmoid(r))).astype(o_ref.dtype)
            kvs.append(_dot_tn(vh, ke2))
        stt_ref[p] = decay[:, ps] * stt + jnp.where(head_lo, kvs[0], kvs[1])


def _own_lanes(w):
    rowid = lax.broadcasted_iota(jnp.int32, (H_SB, w), 0)
    return rowid == lax.broadcasted_iota(jnp.int32, (H_SB, w), 1) // DH_SB


def _spread_query(q_ref):
    w = q_ref.shape[1]
    return jnp.where(_own_lanes(w), jnp.broadcast_to(q_ref[...].astype(F32), (H_SB, w)), 0.0)


def _decode_new_token(q_ref, kn_ref, vn_ref, bias, acc_ref, carry_ref, past_len):
    w = q_ref.shape[1]
    s_pos = past_len + lax.broadcasted_iota(jnp.int32, (H_SB, 1), 1)
    visible = s_pos < past_len
    z = jnp.sum(_spread_query(q_ref) * kn_ref[...], axis=1, keepdims=True) + bias[:, :1]
    l1m = jnp.where(visible, _neg_softplus(z), 0.0)
    att = jnp.where(visible, jnp.exp(z + l1m), 0.0)
    acc_ref[...] = att * jnp.broadcast_to(vn_ref[...], (H_SB, w))
    carry_ref[...] = jnp.broadcast_to(l1m, carry_ref.shape)


def _decode_pages(q_ref, bias, tri, k_buf, v_buf, acc_ref, carry_ref):
    pages = k_buf.shape[0]
    qb = _spread_query(q_ref).astype(BF16)
    z3 = jnp.stack([_dot(qb, k_buf[p].astype(BF16)) for p in range(pages)]) + bias[None]
    l1m = _neg_softplus(z3)
    hi, lo = _split_bf16(l1m.reshape(pages * H_SB, LANES))
    within = (_dot(hi, tri) + _dot(lo, tri)).reshape(pages, H_SB, LANES)
    page_sum = jnp.sum(l1m, axis=2, keepdims=True)
    carry = carry_ref[:, :1]
    carries = []
    for p in range(pages):
        carries.append(carry)
        carry = carry + page_sum[p]
    att = jnp.exp(z3 + l1m + within + jnp.stack(carries)).astype(BF16)
    acc = acc_ref[...]
    for p in range(pages):
        acc = acc + _dot_nt(att[p], v_buf[p].astype(BF16))
    acc_ref[...] = acc
    carry_ref[...] = jnp.broadcast_to(carry, carry_ref.shape)


def _gla_decode_body(pt_ref, gq_ref, gk_ref, gla_ref, gv_ref, gr_ref, gn_ref, tril_ref,
                     q_ref, kn_ref, vn_ref, bias_ref, tri_ref, ck_ref, cv_ref,
                     og_ref, st_ref, osb_ref, stt_ref, acc_ref, carry_ref, kbuf, vbuf, sem,
                     *, pool_offset, past_len, n_seq, n_chunks, dec_steps):
    pages = kbuf.shape[1]
    n_pages = pt_ref.shape[1]
    i = pl.program_id(0)
    n_steps = pl.num_programs(0)
    seq = i % n_seq
    c = i // n_seq
    j = i % dec_steps
    slot = i % 2
    bias = bias_ref[...]

    def page_copies(step, to_slot):
        b = step // dec_steps
        first = n_pages - 1 - (step % dec_steps) * pages
        copies = []
        for p in range(pages):
            page = pool_offset + pt_ref[b, first - p]
            copies.append(pltpu.make_async_copy(ck_ref.at[page], kbuf.at[to_slot, p],
                                                sem.at[to_slot, 0]))
            copies.append(pltpu.make_async_copy(cv_ref.at[page], vbuf.at[to_slot, p],
                                                sem.at[to_slot, 1]))
        return copies

    @pl.when(i == 0)
    def _():
        for cp in page_copies(i, slot):
            cp.start()

    @pl.when(i + 1 < n_steps)
    def _():
        for cp in page_copies(i + 1, 1 - slot):
            cp.start()

    @pl.when(c == 0)
    def _():
        stt_ref[seq] = jnp.zeros(stt_ref.shape[1:], F32)

    @pl.when(j == 0)
    def _():
        _decode_new_token(q_ref, kn_ref, vn_ref, bias, acc_ref, carry_ref, past_len)

    for cp in page_copies(i, slot):
        cp.wait()

    _gla_chunk(gq_ref, gk_ref, gla_ref, gv_ref, gr_ref, gn_ref[...], tril_ref[...], og_ref,
               stt_ref.at[seq])
    _decode_pages(q_ref, bias, tri_ref[...], kbuf.at[slot], vbuf.at[slot], acc_ref, carry_ref)

    @pl.when(c == n_chunks - 1)
    def _():
        for p in range(H_GLA // 2):
            st_ref[seq, p] = stt_ref[seq, p].T

    @pl.when(j == dec_steps - 1)
    def _():
        own = _own_lanes(acc_ref.shape[1])
        osb_ref[...] = jnp.sum(jnp.where(own, acc_ref[...], 0.0), axis=0,
                               keepdims=True).astype(osb_ref.dtype)


def _gla_decode(qg, kg, la, vg, rg, gn, tril,
                q, k_new, v_new, cache_kt, cache_vt, pool_offset, page_table, bias8, tri):
    n_seq, s, _ = qg.shape
    cs = min(GLA_CHUNK, s)
    n_chunks = s // cs
    nb, n_pages = page_table.shape
    _, w, page = cache_kt.shape
    pages = DEC_PAGES_PER_STEP
    dec_steps = n_pages // pages
    past_len = n_pages * page
    n_steps = n_seq * n_chunks
    assert s % cs == 0 and n_pages % pages == 0 and page == LANES
    assert n_steps == nb * dec_steps, "both jobs must have the same number of grid steps"

    chunk = lambda n: pl.BlockSpec((None, cs, n), lambda i, pt: (i % n_seq, i // n_seq, 0))
    const = lambda shape: pl.BlockSpec(shape, lambda i, pt: (0,) * len(shape))
    vec = pl.BlockSpec((None, 1, w), lambda i, pt: (i // dec_steps, 0, 0))

    st_shape = (n_seq, H_GLA // 2, 2 * DK_GLA, DV_GLA)
    og, st, osb = pl.pallas_call(
        functools.partial(_gla_decode_body, pool_offset=pool_offset, past_len=past_len,
                          n_seq=n_seq, n_chunks=n_chunks, dec_steps=dec_steps),
        grid_spec=pltpu.PrefetchScalarGridSpec(
            num_scalar_prefetch=1,
            grid=(n_steps,),
            in_specs=[chunk(WK_GLA), chunk(WK_GLA), chunk(WK_GLA), chunk(WV_GLA), chunk(WV_GLA),
                      const(gn.shape), const(tril.shape),
                      vec, vec, vec, const(bias8.shape), const(tri.shape),
                      pl.BlockSpec(memory_space=pl.ANY), pl.BlockSpec(memory_space=pl.ANY)],
            out_specs=[chunk(WV_GLA), const(st_shape), vec],
            scratch_shapes=[pltpu.VMEM((n_seq, H_GLA // 2, DV_GLA, 2 * DK_GLA), F32),
                            pltpu.VMEM((H_SB, w), F32), pltpu.VMEM((H_SB, LANES), F32),
                            pltpu.VMEM((2, pages, w, page), F32),
                            pltpu.VMEM((2, pages, w, page), F32),
                            pltpu.SemaphoreType.DMA((2, 2))],
        ),
        out_shape=[jax.ShapeDtypeStruct((n_seq, s, WV_GLA), BF16),
                   jax.ShapeDtypeStruct(st_shape, F32),
                   jax.ShapeDtypeStruct((nb, 1, w), BF16)],
        compiler_params=_params("arbitrary"),
        name="gla_decode",
    )(page_table, qg, kg, la, vg, rg, gn, tril, q, k_new, v_new, bias8, tri, cache_kt, cache_vt)
    return og, st.reshape(n_seq, H_GLA, DK_GLA, DV_GLA), osb


def _merge_body(x_ref, osb_ref, og_ref, gt_ref, wosb_ref, wogl_ref, wout_ref, o_ref):
    d = x_ref.shape[1]
    g = gt_ref[...].astype(F32)
    m = g[:, :d] * _dot(osb_ref[...], wosb_ref[...]) + g[:, d:] * _dot(og_ref[...], wogl_ref[...])
    o_ref[...] = x_ref[...] + _dot(m.astype(BF16), wout_ref[...])


def _merge(x, osb, og, gt, w, tm):
    t, d = x.shape
    row = lambda n: pl.BlockSpec((tm, n), lambda i: (i, 0))
    return pl.pallas_call(
        _merge_body,
        grid=(t // tm,),
        in_specs=[row(d), row(W_SB), row(WV_GLA), row(2 * d),
                  _full(w["w_o_sb"].shape), _full(w["w_o_gla"].shape), _full(w["w_out"].shape)],
        out_specs=row(d),
        out_shape=jax.ShapeDtypeStruct((t, d), F32),
        compiler_params=_params("parallel"),
        name="merge",
    )(x, osb, og, gt, w["w_o_sb"], w["w_o_gla"], w["w_out"])


def _gla_step_body(q_ref, k_ref, la_ref, v_ref, r_ref, gn_ref, s0_ref, o_ref, s1_ref):
    nb = q_ref.shape[0]
    rowid = lax.broadcasted_iota(jnp.int32, (8, WK_GLA), 0)
    own = rowid == lax.broadcasted_iota(jnp.int32, (8, WK_GLA), 1) // DK_GLA
    vrow = lax.broadcasted_iota(jnp.int32, (8, DV_GLA), 0)
    ones = jnp.ones((8, DV_GLA), BF16)
    gn = gn_ref[...]

    def rows_of_heads(x):
        out = jnp.broadcast_to(x[:, :DV_GLA], (8, DV_GLA))
        for h in range(1, H_GLA):
            out = jnp.where(vrow == h, jnp.broadcast_to(x[:, h * DV_GLA:(h + 1) * DV_GLA],
                                                        (8, DV_GLA)), out)
        return out

    for i in range(nb):
        q = q_ref[i:i + 1, :]
        k = k_ref[i:i + 1, :]
        decay = jnp.exp(la_ref[i:i + 1, :])
        s0 = s0_ref[i]
        v8 = rows_of_heads(v_ref[i:i + 1, :])
        r8 = rows_of_heads(r_ref[i:i + 1, :])
        spread = lambda x: jnp.where(own, jnp.broadcast_to(x, (8, WK_GLA)), 0.0)
        qk = jnp.sum(spread(q * k), axis=1, keepdims=True)
        o = qk * v8 + _dot(spread(q * decay).astype(BF16), s0.astype(BF16))
        o = _rms(o, gn) * (r8 * jax.nn.sigmoid(r8))
        for h in range(H_GLA):
            o_ref[i:i + 1, h * DV_GLA:(h + 1) * DV_GLA] = o[h:h + 1, :].astype(o_ref.dtype)
        d_hi = decay.astype(BF16).astype(F32)
        dd = jnp.where(rowid == 0, jnp.broadcast_to(d_hi, (8, WK_GLA)),
                       jnp.where(rowid == 1, jnp.broadcast_to(decay - d_hi, (8, WK_GLA)), 0.0))
        decay_col = _dot_tn(dd.astype(BF16), ones)
        s1_ref[i] = decay_col * s0 + _dot_tn(spread(k).astype(BF16), v8.astype(BF16))


def _gla_step(qg, kg, la, vg, rg, gn, s0, layer, nb_blk=8):
    nb = qg.shape[0]
    blk0 = layer * (nb // nb_blk)
    row = lambda n: pl.BlockSpec((nb_blk, n), lambda i: (i, 0))
    st = pl.BlockSpec((nb_blk, WK_GLA, DV_GLA), lambda i: (i, 0, 0))
    st_in = pl.BlockSpec((nb_blk, WK_GLA, DV_GLA), lambda i: (blk0 + i, 0, 0))
    o, s1 = pl.pallas_call(
        _gla_step_body,
        grid=(nb // nb_blk,),
        in_specs=[row(WK_GLA), row(WK_GLA), row(WK_GLA), row(WV_GLA), row(WV_GLA),
                  _full(gn.shape), st_in],
        out_specs=[row(WV_GLA), st],
        out_shape=[jax.ShapeDtypeStruct((nb, WV_GLA), BF16),
                   jax.ShapeDtypeStruct((nb, WK_GLA, DV_GLA), F32)],
        compiler_params=_params("parallel"),
        name="gla_step",
    )(qg, kg, la, vg, rg, gn, s0)
    return o, s1.reshape(nb, H_GLA, DK_GLA, DV_GLA)


def _prep_layer(l, ffn1_norm, ffn1_w_in, ffn1_w_out, mix_norm, w_in, b_gate, q_norm, k_norm,
                sb_bias, w_a2, b_a, gla_norm, w_o_sb, w_o_gla, w_out, ffn2_norm, ffn2_w_in,
                ffn2_w_out):
    d = w_in.shape[1]
    d_ff = ffn1_w_out.shape[1]
    bf = lambda a: a.astype(BF16)
    o_gl = 3 * W_SB
    o_alr = o_gl + 2 * WK_GLA + 2 * WV_GLA
    o_gt = o_alr + GATE_RANK
    win = w_in[l]
    head = jnp.arange(W_SB) // DH_SB
    return {
        "ffn1": (ffn1_norm[l][None], bf(ffn1_w_in[l][:, :d_ff]), bf(ffn1_w_in[l][:, d_ff:]),
                 bf(ffn1_w_out[l])),
        "ffn2": (ffn2_norm[l][None], bf(ffn2_w_in[l][:, :d_ff]), bf(ffn2_w_in[l][:, d_ff:]),
                 bf(ffn2_w_out[l])),
        "mix_norm": mix_norm[l][None],
        "w_sb": bf(win[:, :o_gl]),
        "w_gl": bf(win[:, o_gl:o_alr]),
        "w_alr": bf(jnp.pad(win[:, o_alr:o_gt], ((0, 0), (0, LANES - GATE_RANK)))),
        "w_a2": bf(jnp.pad(w_a2[l], ((0, LANES - GATE_RANK), (0, 0)))),
        "b_a": b_a[l][None],
        "w_gt": bf(win[:, o_gt:]),
        "b_gt": b_gate[l][None],
        "q_norm": jnp.tile(q_norm[l], H_SB)[None],
        "k_norm": jnp.tile(k_norm[l], H_SB)[None],
        "blk": (head[:, None] == head[None, :]).astype(BF16),
        "sb_bias": sb_bias[l],
        "gla_norm": gla_norm[l][None],
        "w_o_sb": bf(w_o_sb[l]),
        "w_o_gla": bf(w_o_gla[l]),
        "w_out": bf(w_out[l]),
    }


def _strict_upper(n):
    i = jnp.arange(n)
    return (i[:, None] > i[None, :]).astype(BF16)


def _layer(xp, xs, cache_k, cache_v, state, page_table, layer, w):
    b, s, d = xp.shape
    nb, s1, _ = xs.shape
    assert s1 == 1
    tm = 512
    depth, n_pool, page = cache_k.shape[:3]

    x2 = _ffn(xp.reshape(b * s, d), *w["ffn1"], 2 * tm)
    q, kt, vt, kb, vb, qg, kg, la, vg, rg, gt = _inproj(x2, w, tm, seq=s)
    y2 = _ffn(xs.reshape(nb, d), *w["ffn1"], nb)
    qd, kd, vd, _, _, qgd, kgd, lad, vgd, rgd, gtd = _inproj(y2, w, nb)

    sh = lambda a: a.reshape(b, s, a.shape[-1])
    osb = _sb_prompt(sh(q), sh(kb), sh(vb), w["sb_bias"], _strict_upper(SB_TILE))

    cs = min(GLA_CHUNK, s)
    i = jnp.arange(cs)
    tril = (i[None, :] <= i[:, None]).astype(BF16)
    bias8 = jnp.broadcast_to(w["sb_bias"][:, None], (H_SB, LANES))
    pool_t = lambda c: jnp.transpose(c, (0, 1, 3, 4, 2)).reshape(depth * n_pool, W_SB, page)
    vec = lambda a: a.reshape(nb, 1, W_SB)
    og, stp, osd = _gla_decode(sh(qg), sh(kg), sh(la), sh(vg), sh(rg), w["gla_norm"], tril,
                               vec(qd), vec(kd), vec(vd), pool_t(cache_k), pool_t(cache_v),
                               layer * n_pool, page_table, bias8, _strict_upper(page))
    ogd, std = _gla_step(qgd, kgd, lad, vgd, rgd, w["gla_norm"],
                         state.reshape(depth * nb, WK_GLA, DV_GLA), layer)

    x2 = _merge(x2, osb.reshape(b * s, W_SB), og.reshape(b * s, WV_GLA), gt, w, 2 * tm)
    x2 = _ffn(x2, *w["ffn2"], 2 * tm)
    y2 = _merge(y2, osd.reshape(nb, W_SB), ogd, gtd, w, nb)
    y2 = _ffn(y2, *w["ffn2"], nb)

    heads = lambda a: jnp.transpose(a.reshape(b, H_SB, DH_SB, s), (0, 3, 1, 2))
    new = lambda a: a.reshape(nb, 1, H_SB, DH_SB)
    return (x2.reshape(b, s, d), y2.reshape(nb, 1, d),
            (heads(kt), heads(vt), stp, new(kd), new(vd), std))


def kernel(x_prompt, x_sample, cache_k, cache_v, state_gla, page_table, ffn1_norm, ffn1_w_in,
           ffn1_w_out, mix_norm, w_in, b_gate, q_norm, k_norm, sb_bias, w_a2, b_a, gla_norm,
           w_o_sb, w_o_gla, w_out, ffn2_norm, ffn2_w_in, ffn2_w_out):
    depth = w_in.shape[0]
    yp, ys = x_prompt, x_sample
    outs = [[] for _ in range(6)]
    for l in range(depth):
        w = _prep_layer(l, ffn1_norm, ffn1_w_in, ffn1_w_out, mix_norm, w_in, b_gate, q_norm,
                        k_norm, sb_bias, w_a2, b_a, gla_norm, w_o_sb, w_o_gla, w_out, ffn2_norm,
                        ffn2_w_in, ffn2_w_out)
        yp, ys, vals = _layer(yp, ys, cache_k, cache_v, state_gla, page_table, l, w)
        for acc, val in zip(outs, vals):
            acc.append(val)
    return (yp, ys) + tuple(jnp.stack(o) for o in outs)
```

```python
import functools

import jax
import jax.numpy as jnp
from jax import lax
from jax.experimental import pallas as pl
from jax.experimental.pallas import tpu as pltpu

F32 = jnp.float32
BF16 = jnp.bfloat16

EPS = 1e-6
H_SB = 8
DH_SB = 64
W_SB = H_SB * DH_SB
H_GLA = 4
DK_GLA = 64
DV_GLA = 128
WK_GLA = H_GLA * DK_GLA
WV_GLA = H_GLA * DV_GLA
GATE_RANK = 16
GATE_TAU = 16.0
GLA_CHUNK = 64

LANES = 128
VMEM_LIMIT_BYTES = 56 * 1024 * 1024
SB_TILE = 256
DEC_PAGES_PER_STEP = 16

_NT = (((1,), (1,)), ((), ()))
_TN = (((0,), (0,)), ((), ()))


def _dot(a, b):
    return jnp.dot(a, b, preferred_element_type=F32)


def _dot_nt(a, b):
    return lax.dot_general(a, b, _NT, preferred_element_type=F32)


def _dot_tn(a, b):
    return lax.dot_general(a, b, _TN, preferred_element_type=F32)


def _rms(x, g):
    return x * lax.rsqrt(jnp.mean(x * x, axis=-1, keepdims=True) + EPS) * g


def _neg_softplus(z):
    return -(jnp.maximum(z, 0.0) + jnp.log1p(jnp.exp(-jnp.abs(z))))


def _split_bf16(x):
    hi = x.astype(BF16)
    lo = (x - hi.astype(F32)).astype(BF16)
    return hi, lo


def _params(*sem):
    return pltpu.CompilerParams(dimension_semantics=sem, vmem_limit_bytes=VMEM_LIMIT_BYTES)


def _full(shape):
    nd = len(shape)
    return pl.BlockSpec(shape, lambda *_: (0,) * nd)


def _ffn_body(x_ref, g_ref, wa_ref, wb_ref, wo_ref, o_ref, *, chunks):
    x = x_ref[...]
    xn = _rms(x, g_ref[...]).astype(BF16)
    acc = None
    for c0, cw in chunks:
        a = _dot(xn, wa_ref[:, c0:c0 + cw])
        b = _dot(xn, wb_ref[:, c0:c0 + cw])
        h = (a * jax.nn.sigmoid(a) * b).astype(BF16)
        y = _dot(h, wo_ref[c0:c0 + cw, :])
        acc = y if acc is None else acc + y
    o_ref[...] = x + 0.5 * acc


def _ffn_chunks(f, width=1024):
    assert f % 256 == 0
    return tuple((c0, min(width, f - c0)) for c0 in range(0, f, width))


def _ffn(x, g, wa, wb, wo, tm):
    t, d = x.shape
    f = wa.shape[1]
    chunks = _ffn_chunks(f)
    row = pl.BlockSpec((tm, d), lambda i: (i, 0))
    return pl.pallas_call(
        functools.partial(_ffn_body, chunks=chunks),
        grid=(t // tm,),
        in_specs=[row, _full((1, d)), _full((d, f)), _full((d, f)), _full((f, d))],
        out_specs=row,
        out_shape=jax.ShapeDtypeStruct((t, d), F32),
        compiler_params=_params("parallel"),
        name="ffn",
    )(x, g, wa, wb, wo)


def _inproj_body(x_ref, g_ref, wsb_ref, wgl_ref, walr_ref, wa2_ref, ba_ref, wgt_ref, bgt_ref,
                 qn_ref, kn_ref, blk_ref,
                 q_ref, k_ref, v_ref, kb_ref, vb_ref, qg_ref, kg_ref, la_ref, vg_ref, rg_ref,
                 gt_ref, *, kv_transposed):
    h = _rms(x_ref[...], g_ref[...]).astype(BF16)

    def head_norm(t, gain):
        ss = _dot((t * t).astype(BF16), blk_ref[...])
        return t * lax.rsqrt(ss * (1.0 / DH_SB) + EPS) * gain

    zsb = _dot(h, wsb_ref[...])
    q = head_norm(zsb[:, :W_SB], qn_ref[...])
    k = head_norm(zsb[:, W_SB:2 * W_SB], kn_ref[...])
    v = zsb[:, 2 * W_SB:]
    q_ref[...] = (q * (DH_SB ** -0.5)).astype(BF16)
    k_ref[...] = k.T if kv_transposed else k
    v_ref[...] = v.T if kv_transposed else v
    kb_ref[...] = k.astype(BF16)
    vb_ref[...] = v.astype(BF16)

    zgl = _dot(h, wgl_ref[...])
    qg_ref[...] = zgl[:, :WK_GLA] * (DK_GLA ** -0.5)
    kg_ref[...] = zgl[:, WK_GLA:2 * WK_GLA]
    vg_ref[...] = zgl[:, 2 * WK_GLA:2 * WK_GLA + WV_GLA]
    rg_ref[...] = zgl[:, 2 * WK_GLA + WV_GLA:]

    alr = _dot(h, walr_ref[...])
    lin = _dot(alr.astype(BF16), wa2_ref[...]) + ba_ref[...]
    la_ref[...] = _neg_softplus(-lin) * (1.0 / GATE_TAU)

    gt_ref[...] = jax.nn.sigmoid(_dot(h, wgt_ref[...]) + bgt_ref[...]).astype(BF16)


def _inproj(x, w, tm, seq=None):
    t, d = x.shape
    row = lambda n: pl.BlockSpec((tm, n), lambda i: (i, 0))
    ins = [x, w["mix_norm"], w["w_sb"], w["w_gl"], w["w_alr"], w["w_a2"], w["b_a"], w["w_gt"],
           w["b_gt"], w["q_norm"], w["k_norm"], w["blk"]]
    in_specs = [row(d)] + [_full(a.shape) for a in ins[1:]]
    outs = [(W_SB, BF16), (W_SB, F32), (W_SB, F32), (W_SB, BF16), (W_SB, BF16),
            (WK_GLA, F32), (WK_GLA, F32), (WK_GLA, F32), (WV_GLA, F32), (WV_GLA, F32),
            (2 * d, BF16)]
    out_specs = [row(n) for n, _ in outs]
    out_shape = [jax.ShapeDtypeStruct((t, n), dt) for n, dt in outs]
    if seq is not None:
        per_seq = seq // tm
        kv_t = pl.BlockSpec((None, W_SB, tm), lambda i: (i // per_seq, 0, i % per_seq))
        out_specs[1] = out_specs[2] = kv_t
        out_shape[1] = out_shape[2] = jax.ShapeDtypeStruct((t // seq, W_SB, seq), F32)
    return pl.pallas_call(
        functools.partial(_inproj_body, kv_transposed=seq is not None),
        grid=(t // tm,),
        in_specs=in_specs,
        out_specs=out_specs,
        out_shape=out_shape,
        compiler_params=_params("parallel"),
        name="inproj",
    )(*ins)


def _sb_prompt_body(bias_ref, q_ref, k_ref, v_ref, tri_ref, o_ref, *, tq):
    pair = pl.program_id(1)
    s = q_ref.shape[0]
    tk = tri_ref.shape[0]
    assert tq == 2 * tk
    lane = lax.broadcasted_iota(jnp.int32, (1, LANES), 1)
    head_lo = lane < DH_SB
    first = lax.broadcasted_iota(jnp.int32, (2 * tk, 1), 0) < tk
    bias2 = jnp.where(first, bias_ref[2 * pair], bias_ref[2 * pair + 1])
    tri = tri_ref[...]
    t_pos = lax.broadcasted_iota(jnp.int32, (2 * tk, tk), 0) & (tk - 1)
    s_pos = lax.broadcasted_iota(jnp.int32, (2 * tk, tk), 1)
    causal = s_pos < t_pos

    def block(qs, start, carry, acc, mask):
        kblk = k_ref[start:start + tk, :]
        vblk = v_ref[start:start + tk, :]
        z = _dot_nt(qs, kblk) + bias2
        sp = jnp.maximum(z, 0.0) + jnp.log(1.0 + jnp.exp(-jnp.abs(z)))
        if mask is not None:
            sp = jnp.where(mask, sp, 0.0)
        later = _dot(sp.astype(BF16), tri)
        att = jnp.exp(z - sp - later - carry)
        if mask is not None:
            att = jnp.where(mask, att, 0.0)
        acc = acc + _dot(att.astype(BF16), vblk)
        carry = carry + jnp.sum(sp, axis=1, keepdims=True)
        return carry, acc

    zc = jnp.zeros((2 * tk, 1), F32)
    za = jnp.zeros((2 * tk, LANES), F32)
    for qi in range(s // tq):
        q0 = qi * tq
        q2 = q_ref[q0:q0 + tq, :]
        zero = jnp.zeros_like(q2)
        qa = jnp.where(head_lo, q2, zero)
        qb = jnp.where(head_lo, zero, q2)
        q_early = jnp.concatenate([qa[:tk], qb[:tk]], axis=0)
        q_late = jnp.concatenate([qa[tk:], qb[tk:]], axis=0)
        c1, a1 = block(q_late, q0 + tk, zc, za, causal)
        c1, a1 = block(q_late, q0, c1, a1, None)
        c0, a0 = block(q_early, q0, zc, za, causal)
        for start in range(q0 - tk, -1, -tk):
            c0, a0 = block(q_early, start, c0, a0, None)
            c1, a1 = block(q_late, start, c1, a1, None)
        o_ref[q0:q0 + tk, :] = jnp.where(head_lo, a0[:tk], a0[tk:]).astype(o_ref.dtype)
        o_ref[q0 + tk:q0 + tq, :] = jnp.where(head_lo, a1[:tk], a1[tk:]).astype(o_ref.dtype)


def _sb_prompt(q, k, v, bias, tri):
    b, s, w = q.shape
    tq = 2 * tri.shape[0]
    spec = pl.BlockSpec((None, s, LANES), lambda bi, p, *_: (bi, 0, p))
    return pl.pallas_call(
        functools.partial(_sb_prompt_body, tq=tq),
        grid_spec=pltpu.PrefetchScalarGridSpec(
            num_scalar_prefetch=1,
            grid=(b, w // LANES),
            in_specs=[spec, spec, spec, pl.BlockSpec(tri.shape, lambda *_: (0, 0))],
            out_specs=spec,
        ),
        out_shape=jax.ShapeDtypeStruct((b, s, w), BF16),
        compiler_params=_params("parallel", "parallel"),
        name="sb_prompt",
    )(bias, q, k, v, tri)


def _gla_chunk(q_ref, k_ref, la_ref, v_ref, r_ref, gn, tril, o_ref, stt_ref):
    cs = q_ref.shape[0]
    lane = lax.broadcasted_iota(jnp.int32, (1, LANES), 1)
    head_lo = lane < DK_GLA
    ti = lax.broadcasted_iota(jnp.int32, (cs, cs), 0)
    tj = lax.broadcasted_iota(jnp.int32, (cs, cs), 1)
    causal = tj <= ti
    la_hi, la_lo = _split_bf16(la_ref[...])
    bc = _dot(tril, la_hi) + _dot(tril, la_lo)
    b_last = bc[cs - 1:cs, :]
    q_dec = q_ref[...] * jnp.exp(bc)
    k_inv = (k_ref[...] * jnp.exp(-bc)).astype(BF16)
    k_end = (k_ref[...] * jnp.exp(b_last - bc)).astype(BF16)
    decay = jnp.exp(b_last)
    vb = v_ref[...].astype(BF16)
    rb = r_ref[...]
    for p in range(H_GLA // 2):
        ps = slice(p * LANES, (p + 1) * LANES)
        qd2 = q_dec[:, ps]
        ki2 = k_inv[:, ps]
        ke2 = k_end[:, ps]
        stt = stt_ref[p]
        stt_b = stt.astype(BF16)
        kvs = []
        for hh in range(2):
            h = 2 * p + hh
            hs = slice(h * DV_GLA, (h + 1) * DV_GLA)
            in_head = head_lo if hh == 0 else jnp.logical_not(head_lo)
            qa = jnp.where(in_head, qd2, 0.0).astype(BF16)
            att = jnp.where(causal, _dot_nt(qa, ki2), 0.0)
            vh = vb[:, hs]
            o = _dot(att.astype(BF16), vh) + _dot_nt(qa, stt_b)
            o = _rms(o, gn)
            r = rb[:, hs]
            o_ref[:, hs] = (o * (r * jax.nn.sigmoid(r))).astype(o_ref.dtype)
            kvs.append(_dot_tn(vh, ke2))
        stt_ref[p] = decay[:, ps] * stt + jnp.where(head_lo, kvs[0], kvs[1])


def _own_lanes(w):
    rowid = lax.broadcasted_iota(jnp.int32, (H_SB, w), 0)
    return rowid == lax.broadcasted_iota(jnp.int32, (H_SB, w), 1) // DH_SB


def _spread_query(q_ref):
    w = q_ref.shape[1]
    return jnp.where(_own_lanes(w), jnp.broadcast_to(q_ref[...].astype(F32), (H_SB, w)), 0.0)


def _decode_new_token(q_ref, kn_ref, vn_ref, bias, acc_ref, carry_ref, past_len):
    w = q_ref.shape[1]
    s_pos = past_len + lax.broadcasted_iota(jnp.int32, (H_SB, 1), 1)
    visible = s_pos < past_len
    z = jnp.sum(_spread_query(q_ref) * kn_ref[...], axis=1, keepdims=True) + bias[:, :1]
    l1m = jnp.where(visible, _neg_softplus(z), 0.0)
    att = jnp.where(visible, jnp.exp(z + l1m), 0.0)
    acc_ref[...] = att * jnp.broadcast_to(vn_ref[...], (H_SB, w))
    carry_ref[...] = jnp.broadcast_to(l1m, carry_ref.shape)


def _decode_pages(q_ref, bias, tri, k_buf, v_buf, acc_ref, carry_ref):
    pages = k_buf.shape[0]
    qb = _spread_query(q_ref).astype(BF16)
    z3 = jnp.stack([_dot(qb, k_buf[p].astype(BF16)) for p in range(pages)]) + bias[None]
    l1m = _neg_softplus(z3)
    hi, lo = _split_bf16(l1m.reshape(pages * H_SB, LANES))
    within = (_dot(hi, tri) + _dot(lo, tri)).reshape(pages, H_SB, LANES)
    page_sum = jnp.sum(l1m, axis=2, keepdims=True)
    carry = carry_ref[:, :1]
    carries = []
    for p in range(pages):
        carries.append(carry)
        carry = carry + page_sum[p]
    att = jnp.exp(z3 + l1m + within + jnp.stack(carries)).astype(BF16)
    acc = acc_ref[...]
    for p in range(pages):
        acc = acc + _dot_nt(att[p], v_buf[p].astype(BF16))
    acc_ref[...] = acc
    carry_ref[...] = jnp.broadcast_to(carry, carry_ref.shape)


def _gla_decode_body(pt_ref, gq_ref, gk_ref, gla_ref, gv_ref, gr_ref, gn_ref, tril_ref,
                     q_ref, kn_ref, vn_ref, bias_ref, tri_ref, ck_ref, cv_ref,
                     og_ref, st_ref, osb_ref, stt_ref, acc_ref, carry_ref, kbuf, vbuf, sem,
                     *, pool_offset, past_len, n_seq, n_chunks, dec_steps):
    pages = kbuf.shape[1]
    n_pages = pt_ref.shape[1]
    i = pl.program_id(0)
    n_steps = pl.num_programs(0)
    seq = i % n_seq
    c = i // n_seq
    j = i % dec_steps
    slot = i % 2
    bias = bias_ref[...]

    def page_copies(step, to_slot):
        b = step // dec_steps
        first = n_pages - 1 - (step % dec_steps) * pages
        copies = []
        for p in range(pages):
            page = pool_offset + pt_ref[b, first - p]
            copies.append(pltpu.make_async_copy(ck_ref.at[page], kbuf.at[to_slot, p],
                                                sem.at[to_slot, 0]))
            copies.append(pltpu.make_async_copy(cv_ref.at[page], vbuf.at[to_slot, p],
                                                sem.at[to_slot, 1]))
        return copies

    def start_all(copies):
        for n, cp in enumerate(copies):
            cp.start(priority=n % 2)

    @pl.when(i == 0)
    def _():
        start_all(page_copies(i, slot))

    @pl.when(i + 1 < n_steps)
    def _():
        start_all(page_copies(i + 1, 1 - slot))

    @pl.when(c == 0)
    def _():
        stt_ref[seq] = jnp.zeros(stt_ref.shape[1:], F32)

    @pl.when(j == 0)
    def _():
        _decode_new_token(q_ref, kn_ref, vn_ref, bias, acc_ref, carry_ref, past_len)

    for cp in page_copies(i, slot):
        cp.wait()

    _gla_chunk(gq_ref, gk_ref, gla_ref, gv_ref, gr_ref, gn_ref[...], tril_ref[...], og_ref,
               stt_ref.at[seq])
    _decode_pages(q_ref, bias, tri_ref[...], kbuf.at[slot], vbuf.at[slot], acc_ref, carry_ref)

    @pl.when(c == n_chunks - 1)
    def _():
        for p in range(H_GLA // 2):
            st_ref[seq, p] = stt_ref[seq, p].T

    @pl.when(j == dec_steps - 1)
    def _():
        own = _own_lanes(acc_ref.shape[1])
        osb_ref[...] = jnp.sum(jnp.where(own, acc_ref[...], 0.0), axis=0,
                               keepdims=True).astype(osb_ref.dtype)


def _gla_decode(qg, kg, la, vg, rg, gn, tril,
                q, k_new, v_new, cache_kt, cache_vt, pool_offset, page_table, bias8, tri):
    n_seq, s, _ = qg.shape
    cs = min(GLA_CHUNK, s)
    n_chunks = s // cs
    nb, n_pages = page_table.shape
    _, w, page = cache_kt.shape
    pages = DEC_PAGES_PER_STEP
    dec_steps = n_pages // pages
    past_len = n_pages * page
    n_steps = n_seq * n_chunks
    assert s % cs == 0 and n_pages % pages == 0 and page == LANES
    assert n_steps == nb * dec_steps, "both jobs must have the same number of grid steps"

    chunk = lambda n: pl.BlockSpec((None, cs, n), lambda i, pt: (i % n_seq, i // n_seq, 0))
    const = lambda shape: pl.BlockSpec(shape, lambda i, pt: (0,) * len(shape))
    vec = pl.BlockSpec((None, 1, w), lambda i, pt: (i // dec_steps, 0, 0))

    st_shape = (n_seq, H_GLA // 2, 2 * DK_GLA, DV_GLA)
    og, st, osb = pl.pallas_call(
        functools.partial(_gla_decode_body, pool_offset=pool_offset, past_len=past_len,
                          n_seq=n_seq, n_chunks=n_chunks, dec_steps=dec_steps),
        grid_spec=pltpu.PrefetchScalarGridSpec(
            num_scalar_prefetch=1,
            grid=(n_steps,),
            in_specs=[chunk(WK_GLA), chunk(WK_GLA), chunk(WK_GLA), chunk(WV_GLA), chunk(WV_GLA),
                      const(gn.shape), const(tril.shape),
                      vec, vec, vec, const(bias8.shape), const(tri.shape),
                      pl.BlockSpec(memory_space=pl.ANY), pl.BlockSpec(memory_space=pl.ANY)],
            out_specs=[chunk(WV_GLA), const(st_shape), vec],
            scratch_shapes=[pltpu.VMEM((n_seq, H_GLA // 2, DV_GLA, 2 * DK_GLA), F32),
                            pltpu.VMEM((H_SB, w), F32), pltpu.VMEM((H_SB, LANES), F32),
                            pltpu.VMEM((2, pages, w, page), F32),
                            pltpu.VMEM((2, pages, w, page), F32),
                            pltpu.SemaphoreType.DMA((2, 2))],
        ),
        out_shape=[jax.ShapeDtypeStruct((n_seq, s, WV_GLA), BF16),
                   jax.ShapeDtypeStruct(st_shape, F32),
                   jax.ShapeDtypeStruct((nb, 1, w), BF16)],
        compiler_params=_params("arbitrary"),
        name="gla_decode",
    )(page_table, qg, kg, la, vg, rg, gn, tril, q, k_new, v_new, bias8, tri, cache_kt, cache_vt)
    return og, st.reshape(n_seq, H_GLA, DK_GLA, DV_GLA), osb


def _merge_body(x_ref, osb_ref, og_ref, gt_ref, wosb_ref, wogl_ref, wout_ref, o_ref):
    d = x_ref.shape[1]
    g = gt_ref[...].astype(F32)
    m = g[:, :d] * _dot(osb_ref[...], wosb_ref[...]) + g[:, d:] * _dot(og_ref[...], wogl_ref[...])
    o_ref[...] = x_ref[...] + _dot(m.astype(BF16), wout_ref[...])


def _merge(x, osb, og, gt, w, tm):
    t, d = x.shape
    row = lambda n: pl.BlockSpec((tm, n), lambda i: (i, 0))
    return pl.pallas_call(
        _merge_body,
        grid=(t // tm,),
        in_specs=[row(d), row(W_SB), row(WV_GLA), row(2 * d),
                  _full(w["w_o_sb"].shape), _full(w["w_o_gla"].shape), _full(w["w_out"].shape)],
        out_specs=row(d),
        out_shape=jax.ShapeDtypeStruct((t, d), F32),
        compiler_params=_params("parallel"),
        name="merge",
    )(x, osb, og, gt, w["w_o_sb"], w["w_o_gla"], w["w_out"])


def _gla_step_body(q_ref, k_ref, la_ref, v_ref, r_ref, gn_ref, s0_ref, o_ref, s1_ref):
    nb = q_ref.shape[0]
    rowid = lax.broadcasted_iota(jnp.int32, (8, WK_GLA), 0)
    own = rowid == lax.broadcasted_iota(jnp.int32, (8, WK_GLA), 1) // DK_GLA
    vrow = lax.broadcasted_iota(jnp.int32, (8, DV_GLA), 0)
    ones = jnp.ones((8, DV_GLA), BF16)
    gn = gn_ref[...]

    def rows_of_heads(x):
        out = jnp.broadcast_to(x[:, :DV_GLA], (8, DV_GLA))
        for h in range(1, H_GLA):
            out = jnp.where(vrow == h, jnp.broadcast_to(x[:, h * DV_GLA:(h + 1) * DV_GLA],
                                                        (8, DV_GLA)), out)
        return out

    for i in range(nb):
        q = q_ref[i:i + 1, :]
        k = k_ref[i:i + 1, :]
        decay = jnp.exp(la_ref[i:i + 1, :])
        s0 = s0_ref[i]
        v8 = rows_of_heads(v_ref[i:i + 1, :])
        r8 = rows_of_heads(r_ref[i:i + 1, :])
        spread = lambda x: jnp.where(own, jnp.broadcast_to(x, (8, WK_GLA)), 0.0)
        qk = jnp.sum(spread(q * k), axis=1, keepdims=True)
        o = qk * v8 + _dot(spread(q * decay).astype(BF16), s0.astype(BF16))
        o = _rms(o, gn) * (r8 * jax.nn.sigmoid(r8))
        for h in range(H_GLA):
            o_ref[i:i + 1, h * DV_GLA:(h + 1) * DV_GLA] = o[h:h + 1, :].astype(o_ref.dtype)
        d_hi = decay.astype(BF16).astype(F32)
        dd = jnp.where(rowid == 0, jnp.broadcast_to(d_hi, (8, WK_GLA)),
                       jnp.where(rowid == 1, jnp.broadcast_to(decay - d_hi, (8, WK_GLA)), 0.0))
        decay_col = _dot_tn(dd.astype(BF16), ones)
        s1_ref[i] = decay_col * s0 + _dot_tn(spread(k).astype(BF16), v8.astype(BF16))


def _gla_step(qg, kg, la, vg, rg, gn, s0, layer, nb_blk=8):
    nb = qg.shape[0]
    blk0 = layer * (nb // nb_blk)
    row = lambda n: pl.BlockSpec((nb_blk, n), lambda i: (i, 0))
    st = pl.BlockSpec((nb_blk, WK_GLA, DV_GLA), lambda i: (i, 0, 0))
    st_in = pl.BlockSpec((nb_blk, WK_GLA, DV_GLA), lambda i: (blk0 + i, 0, 0))
    o, s1 = pl.pallas_call(
        _gla_step_body,
        grid=(nb // nb_blk,),
        in_specs=[row(WK_GLA), row(WK_GLA), row(WK_GLA), row(WV_GLA), row(WV_GLA),
                  _full(gn.shape), st_in],
        out_specs=[row(WV_GLA), st],
        out_shape=[jax.ShapeDtypeStruct((nb, WV_GLA), BF16),
                   jax.ShapeDtypeStruct((nb, WK_GLA, DV_GLA), F32)],
        compiler_params=_params("parallel"),
        name="gla_step",
    )(qg, kg, la, vg, rg, gn, s0)
    return o, s1.reshape(nb, H_GLA, DK_GLA, DV_GLA)


def _prep_layer(l, ffn1_norm, ffn1_w_in, ffn1_w_out, mix_norm, w_in, b_gate, q_norm, k_norm,
                sb_bias, w_a2, b_a, gla_norm, w_o_sb, w_o_gla, w_out, ffn2_norm, ffn2_w_in,
                ffn2_w_out):
    d = w_in.shape[1]
    d_ff = ffn1_w_out.shape[1]
    bf = lambda a: a.astype(BF16)
    o_gl = 3 * W_SB
    o_alr = o_gl + 2 * WK_GLA + 2 * WV_GLA
    o_gt = o_alr + GATE_RANK
    win = w_in[l]
    head = jnp.arange(W_SB) // DH_SB
    return {
        "ffn1": (ffn1_norm[l][None], bf(ffn1_w_in[l][:, :d_ff]), bf(ffn1_w_in[l][:, d_ff:]),
                 bf(ffn1_w_out[l])),
        "ffn2": (ffn2_norm[l][None], bf(ffn2_w_in[l][:, :d_ff]), bf(ffn2_w_in[l][:, d_ff:]),
                 bf(ffn2_w_out[l])),
        "mix_norm": mix_norm[l][None],
        "w_sb": bf(win[:, :o_gl]),
        "w_gl": bf(win[:, o_gl:o_alr]),
        "w_alr": bf(jnp.pad(win[:, o_alr:o_gt], ((0, 0), (0, LANES - GATE_RANK)))),
        "w_a2": bf(jnp.pad(w_a2[l], ((0, LANES - GATE_RANK), (0, 0)))),
        "b_a": b_a[l][None],
        "w_gt": bf(win[:, o_gt:]),
        "b_gt": b_gate[l][None],
        "q_norm": jnp.tile(q_norm[l], H_SB)[None],
        "k_norm": jnp.tile(k_norm[l], H_SB)[None],
        "blk": (head[:, None] == head[None, :]).astype(BF16),
        "sb_bias": sb_bias[l],
        "gla_norm": gla_norm[l][None],
        "w_o_sb": bf(w_o_sb[l]),
        "w_o_gla": bf(w_o_gla[l]),
        "w_out": bf(w_out[l]),
    }


def _strict_upper(n):
    i = jnp.arange(n)
    return (i[:, None] > i[None, :]).astype(BF16)


def _layer(xp, xs, cache_k, cache_v, state, page_table, layer, w):
    b, s, d = xp.shape
    nb, s1, _ = xs.shape
    assert s1 == 1
    tm = 512
    depth, n_pool, page = cache_k.shape[:3]

    x2 = _ffn(xp.reshape(b * s, d), *w["ffn1"], 2 * tm)
    q, kt, vt, kb, vb, qg, kg, la, vg, rg, gt = _inproj(x2, w, tm, seq=s)
    y2 = _ffn(xs.reshape(nb, d), *w["ffn1"], nb)
    qd, kd, vd, _, _, qgd, kgd, lad, vgd, rgd, gtd = _inproj(y2, w, nb)

    sh = lambda a: a.reshape(b, s, a.shape[-1])
    osb = _sb_prompt(sh(q), sh(kb), sh(vb), w["sb_bias"], _strict_upper(SB_TILE))

    cs = min(GLA_CHUNK, s)
    i = jnp.arange(cs)
    tril = (i[None, :] <= i[:, None]).astype(BF16)
    bias8 = jnp.broadcast_to(w["sb_bias"][:, None], (H_SB, LANES))
    pool_t = lambda c: jnp.transpose(c, (0, 1, 3, 4, 2)).reshape(depth * n_pool, W_SB, page)
    vec = lambda a: a.reshape(nb, 1, W_SB)
    og, stp, osd = _gla_decode(sh(qg), sh(kg), sh(la), sh(vg), sh(rg), w["gla_norm"], tril,
                               vec(qd), vec(kd), vec(vd), pool_t(cache_k), pool_t(cache_v),
                               layer * n_pool, page_table, bias8, _strict_upper(page))
    ogd, std = _gla_step(qgd, kgd, lad, vgd, rgd, w["gla_norm"],
                         state.reshape(depth * nb, WK_GLA, DV_GLA), layer)

    x2 = _merge(x2, osb.reshape(b * s, W_SB), og.reshape(b * s, WV_GLA), gt, w, 2 * tm)
    x2 = _ffn(x2, *w["ffn2"], 2 * tm)
    y2 = _merge(y2, osd.reshape(nb, W_SB), ogd, gtd, w, nb)
    y2 = _ffn(y2, *w["ffn2"], nb)

    heads = lambda a: jnp.transpose(a.reshape(b, H_SB, DH_SB, s), (0, 3, 1, 2))
    new = lambda a: a.reshape(nb, 1, H_SB, DH_SB)
    return (x2.reshape(b, s, d), y2.reshape(nb, 1, d),
            (heads(kt), heads(vt), stp, new(kd), new(vd), std))


def kernel(x_prompt, x_sample, cache_k, cache_v, state_gla, page_table, ffn1_norm, ffn1_w_in,
           ffn1_w_out, mix_norm, w_in, b_gate, q_norm, k_norm, sb_bias, w_a2, b_a, gla_norm,
           w_o_sb, w_o_gla, w_out, ffn2_norm, ffn2_w_in, ffn2_w_out):
    depth = w_in.shape[0]
    yp, ys = x_prompt, x_sample
    outs = [[] for _ in range(6)]
    for l in range(depth):
        w = _prep_layer(l, ffn1_norm, ffn1_w_in, ffn1_w_out, mix_norm, w_in, b_gate, q_norm,
                        k_norm, sb_bias, w_a2, b_a, gla_norm, w_o_sb, w_o_gla, w_out, ffn2_norm,
                        ffn2_w_in, ffn2_w_out)
        yp, ys, vals = _layer(yp, ys, cache_k, cache_v, state_gla, page_table, l, w)
        for acc, val in zip(outs, vals):
            acc.append(val)
    return (yp, ys) + tuple(jnp.stack(o) for o in outs)
```

```python
import functools

import jax
import jax.numpy as jnp
from jax import lax
from jax.experimental import pallas as pl
from jax.experimental.pallas import tpu as pltpu

F32 = jnp.float32
BF16 = jnp.bfloat16

EPS = 1e-6
LOG2E = 1.4426950408889634
H_SB = 8
DH_SB = 64
W_SB = H_SB * DH_SB
H_GLA = 4
DK_GLA = 64
DV_GLA = 128
WK_GLA = H_GLA * DK_GLA
WV_GLA = H_GLA * DV_GLA
GATE_RANK = 16
GATE_TAU = 16.0
GLA_CHUNK = 64

LANES = 128
VMEM_LIMIT_BYTES = 56 * 1024 * 1024
SB_TILE = 256
DEC_PAGES_PER_STEP = 16

_NT = (((1,), (1,)), ((), ()))
_TN = (((0,), (0,)), ((), ()))


def _dot(a, b):
    return jnp.dot(a, b, preferred_element_type=F32)


def _dot_nt(a, b):
    return lax.dot_general(a, b, _NT, preferred_element_type=F32)


def _dot_tn(a, b):
    return lax.dot_general(a, b, _TN, preferred_element_type=F32)


def _rms(x, g):
    return x * lax.rsqrt(jnp.mean(x * x, axis=-1, keepdims=True) + EPS) * g


def _neg_softplus(z):
    return -(jnp.maximum(z, 0.0) + jnp.log1p(jnp.exp(-jnp.abs(z))))


def _split_bf16(x):
    hi = x.astype(BF16)
    lo = (x - hi.astype(F32)).astype(BF16)
    return hi, lo


def _params(*sem):
    return pltpu.CompilerParams(dimension_semantics=sem, vmem_limit_bytes=VMEM_LIMIT_BYTES)


def _full(shape):
    nd = len(shape)
    return pl.BlockSpec(shape, lambda *_: (0,) * nd)


def _ffn_body(x_ref, g_ref, wa_ref, wb_ref, wo_ref, o_ref, *, chunks):
    x = x_ref[...]
    xn = _rms(x, g_ref[...]).astype(BF16)
    acc = None
    for c0, cw in chunks:
        a = _dot(xn, wa_ref[:, c0:c0 + cw])
        b = _dot(xn, wb_ref[:, c0:c0 + cw])
        h = (a * jax.nn.sigmoid(a) * b).astype(BF16)
        y = _dot(h, wo_ref[c0:c0 + cw, :])
        acc = y if acc is None else acc + y
    o_ref[...] = x + 0.5 * acc


def _ffn_chunks(f, width=1024):
    assert f % 256 == 0
    return tuple((c0, min(width, f - c0)) for c0 in range(0, f, width))


def _ffn(x, g, wa, wb, wo, tm):
    t, d = x.shape
    f = wa.shape[1]
    chunks = _ffn_chunks(f)
    row = pl.BlockSpec((tm, d), lambda i: (i, 0))
    return pl.pallas_call(
        functools.partial(_ffn_body, chunks=chunks),
        grid=(t // tm,),
        in_specs=[row, _full((1, d)), _full((d, f)), _full((d, f)), _full((f, d))],
        out_specs=row,
        out_shape=jax.ShapeDtypeStruct((t, d), F32),
        compiler_params=_params("parallel"),
        name="ffn",
    )(x, g, wa, wb, wo)


def _inproj_body(x_ref, g_ref, wsb_ref, wgl_ref, walr_ref, wa2_ref, ba_ref, wgt_ref, bgt_ref,
                 qn_ref, kn_ref, blk_ref,
                 q_ref, k_ref, v_ref, kb_ref, vb_ref, qg_ref, kg_ref, la_ref, vg_ref, rg_ref,
                 gt_ref, *, kv_transposed):
    h = _rms(x_ref[...], g_ref[...]).astype(BF16)

    def head_norm(t, gain):
        ss = _dot((t * t).astype(BF16), blk_ref[...])
        return t * lax.rsqrt(ss * (1.0 / DH_SB) + EPS) * gain

    zsb = _dot(h, wsb_ref[...])
    q = head_norm(zsb[:, :W_SB], qn_ref[...])
    k = head_norm(zsb[:, W_SB:2 * W_SB], kn_ref[...])
    v = zsb[:, 2 * W_SB:]
    q_ref[...] = (q * (DH_SB ** -0.5)).astype(BF16)
    k_ref[...] = k.T if kv_transposed else k
    v_ref[...] = v.T if kv_transposed else v
    kb_ref[...] = k.astype(BF16)
    vb_ref[...] = v.astype(BF16)

    zgl = _dot(h, wgl_ref[...])
    qg_ref[...] = zgl[:, :WK_GLA] * (DK_GLA ** -0.5)
    kg_ref[...] = zgl[:, WK_GLA:2 * WK_GLA]
    vg_ref[...] = zgl[:, 2 * WK_GLA:2 * WK_GLA + WV_GLA]
    rg_ref[...] = zgl[:, 2 * WK_GLA + WV_GLA:]

    alr = _dot(h, walr_ref[...])
    lin = _dot(alr.astype(BF16), wa2_ref[...]) + ba_ref[...]
    la_ref[...] = _neg_softplus(-lin) * (1.0 / GATE_TAU)

    gt_ref[...] = jax.nn.sigmoid(_dot(h, wgt_ref[...]) + bgt_ref[...]).astype(BF16)


def _inproj(x, w, tm, seq=None):
    t, d = x.shape
    row = lambda n: pl.BlockSpec((tm, n), lambda i: (i, 0))
    ins = [x, w["mix_norm"], w["w_sb"], w["w_gl"], w["w_alr"], w["w_a2"], w["b_a"], w["w_gt"],
           w["b_gt"], w["q_norm"], w["k_norm"], w["blk"]]
    in_specs = [row(d)] + [_full(a.shape) for a in ins[1:]]
    outs = [(W_SB, BF16), (W_SB, F32), (W_SB, F32), (W_SB, BF16), (W_SB, BF16),
            (WK_GLA, F32), (WK_GLA, F32), (WK_GLA, F32), (WV_GLA, F32), (WV_GLA, F32),
            (2 * d, BF16)]
    out_specs = [row(n) for n, _ in outs]
    out_shape = [jax.ShapeDtypeStruct((t, n), dt) for n, dt in outs]
    if seq is not None:
        per_seq = seq // tm
        kv_t = pl.BlockSpec((None, W_SB, tm), lambda i: (i // per_seq, 0, i % per_seq))
        out_specs[1] = out_specs[2] = kv_t
        out_shape[1] = out_shape[2] = jax.ShapeDtypeStruct((t // seq, W_SB, seq), F32)
    return pl.pallas_call(
        functools.partial(_inproj_body, kv_transposed=seq is not None),
        grid=(t // tm,),
        in_specs=in_specs,
        out_specs=out_specs,
        out_shape=out_shape,
        compiler_params=_params("parallel"),
        name="inproj",
    )(*ins)


def _sb_unit(bias_ref, pair, q_ref, k_ref, v_ref, tri_ref, o_ref):
    s = q_ref.shape[0]
    tk = tri_ref.shape[0]
    tq = 2 * tk
    lane = lax.broadcasted_iota(jnp.int32, (1, LANES), 1)
    head_lo = lane < DH_SB
    first = lax.broadcasted_iota(jnp.int32, (2 * tk, 1), 0) < tk
    bias2 = jnp.where(first, bias_ref[2 * pair], bias_ref[2 * pair + 1])
    tri = tri_ref[...]
    t_pos = lax.broadcasted_iota(jnp.int32, (2 * tk, tk), 0) & (tk - 1)
    s_pos = lax.broadcasted_iota(jnp.int32, (2 * tk, tk), 1)
    causal = s_pos < t_pos

    def block(qs, start, carry, acc, mask):
        kblk = k_ref[start:start + tk, :]
        vblk = v_ref[start:start + tk, :]
        z = _dot_nt(qs, kblk) + bias2
        sp = jnp.maximum(z, 0.0) + jnp.log(1.0 + jnp.exp2(jnp.abs(z) * -LOG2E))
        if mask is not None:
            sp = jnp.where(mask, sp, 0.0)
        sp_b = sp.astype(BF16)
        later = _dot(sp_b, tri)
        att = jnp.exp(z - sp - later - carry)
        if mask is not None:
            att = jnp.where(mask, att, 0.0)
        acc = acc + _dot(att.astype(BF16), vblk)
        carry = carry + later[:, :1] + sp_b[:, :1].astype(F32)
        return carry, acc

    zc = jnp.zeros((2 * tk, 1), F32)
    za = jnp.zeros((2 * tk, LANES), F32)
    for qi in range(s // tq):
        q0 = qi * tq
        q2 = q_ref[q0:q0 + tq, :]
        zero = jnp.zeros_like(q2)
        qa = jnp.where(head_lo, q2, zero)
        qb = jnp.where(head_lo, zero, q2)
        q_early = jnp.concatenate([qa[:tk], qb[:tk]], axis=0)
        q_late = jnp.concatenate([qa[tk:], qb[tk:]], axis=0)
        c1, a1 = block(q_late, q0 + tk, zc, za, causal)
        c1, a1 = block(q_late, q0, c1, a1, None)
        c0, a0 = block(q_early, q0, zc, za, causal)
        for start in range(q0 - tk, -1, -tk):
            c0, a0 = block(q_early, start, c0, a0, None)
            c1, a1 = block(q_late, start, c1, a1, None)
        o_ref[q0:q0 + tk, :] = jnp.where(head_lo, a0[:tk], a0[tk:]).astype(o_ref.dtype)
        o_ref[q0 + tk:q0 + tq, :] = jnp.where(head_lo, a1[:tk], a1[tk:]).astype(o_ref.dtype)


def _gla_chunk(q_ref, k_ref, la_ref, v_ref, r_ref, gn, tril, o_ref, stt_ref):
    cs = q_ref.shape[0]
    lane = lax.broadcasted_iota(jnp.int32, (1, LANES), 1)
    head_lo = lane < DK_GLA
    ti = lax.broadcasted_iota(jnp.int32, (cs, cs), 0)
    tj = lax.broadcasted_iota(jnp.int32, (cs, cs), 1)
    causal = tj <= ti
    la_hi, la_lo = _split_bf16(la_ref[...])
    bc = _dot(tril, la_hi) + _dot(tril, la_lo)
    b_last = bc[cs - 1:cs, :]
    q_dec = q_ref[...] * jnp.exp(bc)
    k_inv = (k_ref[...] * jnp.exp(-bc)).astype(BF16)
    k_end = (k_ref[...] * jnp.exp(b_last - bc)).astype(BF16)
    decay = jnp.exp(b_last)
    vb = v_ref[...].astype(BF16)
    rb = r_ref[...]
    for p in range(H_GLA // 2):
        ps = slice(p * LANES, (p + 1) * LANES)
        qd2 = q_dec[:, ps]
        ki2 = k_inv[:, ps]
        ke2 = k_end[:, ps]
        stt = stt_ref[p]
        stt_b = stt.astype(BF16)
        kvs = []
        for hh in range(2):
            h = 2 * p + hh
            hs = slice(h * DV_GLA, (h + 1) * DV_GLA)
            in_head = head_lo if hh == 0 else jnp.logical_not(head_lo)
            qa = jnp.where(in_head, qd2, 0.0).astype(BF16)
            att = jnp.where(causal, _dot_nt(qa, ki2), 0.0)
            vh = vb[:, hs]
            o = _dot(att.astype(BF16), vh) + _dot_nt(qa, stt_b)
            o = _rms(o, gn)
            r = rb[:, hs]
            o_ref[:, hs] = (o * (r * jax.nn.sigmoid(r))).astype(o_ref.dtype)
            kvs.append(_dot_tn(vh, ke2))
        stt_ref[p] = decay[:, ps] * stt + jnp.where(head_lo, kvs[0], kvs[1])


def _own_lanes(w):
    rowid = lax.broadcasted_iota(jnp.int32, (H_SB, w), 0)
    return rowid == lax.broadcasted_iota(jnp.int32, (H_SB, w), 1) // DH_SB


def _spread_query(q_ref):
    w = q_ref.shape[1]
    return jnp.where(_own_lanes(w), jnp.broadcast_to(q_ref[...].astype(F32), (H_SB, w)), 0.0)


def _decode_new_token(q_ref, kn_ref, vn_ref, bias, acc_ref, carry_ref, past_len):
    w = q_ref.shape[1]
    s_pos = past_len + lax.broadcasted_iota(jnp.int32, (H_SB, 1), 1)
    visible = s_pos < past_len
    z = jnp.sum(_spread_query(q_ref) * kn_ref[...], axis=1, keepdims=True) + bias[:, :1]
    l1m = jnp.where(visible, _neg_softplus(z), 0.0)
    att = jnp.where(visible, jnp.exp(z + l1m), 0.0)
    acc_ref[...] = att * jnp.broadcast_to(vn_ref[...], (H_SB, w))
    carry_ref[...] = jnp.broadcast_to(l1m, carry_ref.shape)


def _decode_pages(q_ref, bias, tri, k_buf, v_buf, acc_ref, carry_ref):
    pages = k_buf.shape[0]
    qb = _spread_query(q_ref).astype(BF16)
    z3 = jnp.stack([_dot(qb, k_buf[p].astype(BF16)) for p in range(pages)]) + bias[None]
    l1m = _neg_softplus(z3)
    hi, lo = _split_bf16(l1m.reshape(pages * H_SB, LANES))
    within = (_dot(hi, tri) + _dot(lo, tri)).reshape(pages, H_SB, LANES)
    page_sum = jnp.sum(l1m, axis=2, keepdims=True)
    carry = carry_ref[:, :1]
    carries = []
    for p in range(pages):
        carries.append(carry)
        carry = carry + page_sum[p]
    att = jnp.exp(z3 + l1m + within + jnp.stack(carries)).astype(BF16)
    acc = acc_ref[...]
    for p in range(pages):
        acc = acc + _dot_nt(att[p], v_buf[p].astype(BF16))
    acc_ref[...] = acc
    carry_ref[...] = jnp.broadcast_to(carry, carry_ref.shape)


def _gla_decode_body(pt_ref, sbb_ref, gq_ref, gk_ref, gla_ref, gv_ref, gr_ref, gn_ref, tril_ref,
                     q_ref, kn_ref, vn_ref, bias_ref, tri_ref, ck_ref, cv_ref,
                     pq_ref, pk_ref, pv_ref, ptri_ref,
                     og_ref, st_ref, osb_ref, po_ref, stt_ref, acc_ref, carry_ref, kbuf, vbuf, sem,
                     *, pool_offset, past_len, n_seq, n_chunks, dec_steps, sb_every, sb_pairs):
    pages = kbuf.shape[1]
    n_pages = pt_ref.shape[1]
    i = pl.program_id(0)
    n_steps = pl.num_programs(0)
    seq = i % n_seq
    c = i // n_seq
    j = i % dec_steps
    slot = i % 2
    bias = bias_ref[...]

    def page_copies(step, to_slot):
        b = step // dec_steps
        first = n_pages - 1 - (step % dec_steps) * pages
        copies = []
        for p in range(pages):
            page = pool_offset + pt_ref[b, first - p]
            copies.append(pltpu.make_async_copy(ck_ref.at[page], kbuf.at[to_slot, p],
                                                sem.at[to_slot, 0]))
            copies.append(pltpu.make_async_copy(cv_ref.at[page], vbuf.at[to_slot, p],
                                                sem.at[to_slot, 1]))
        return copies

    def start_all(copies):
        for n, cp in enumerate(copies):
            cp.start(priority=n % 2)

    @pl.when(i == 0)
    def _():
        start_all(page_copies(i, slot))

    @pl.when(i + 1 < n_steps)
    def _():
        start_all(page_copies(i + 1, 1 - slot))

    @pl.when(i % sb_every == 0)
    def _():
        _sb_unit(sbb_ref, (i // sb_every) % sb_pairs, pq_ref, pk_ref, pv_ref, ptri_ref, po_ref)

    @pl.when(c == 0)
    def _():
        stt_ref[seq] = jnp.zeros(stt_ref.shape[1:], F32)

    @pl.when(j == 0)
    def _():
        _decode_new_token(q_ref, kn_ref, vn_ref, bias, acc_ref, carry_ref, past_len)

    for cp in page_copies(i, slot):
        cp.wait()

    _gla_chunk(gq_ref, gk_ref, gla_ref, gv_ref, gr_ref, gn_ref[...], tril_ref[...], og_ref,
               stt_ref.at[seq])
    _decode_pages(q_ref, bias, tri_ref[...], kbuf.at[slot], vbuf.at[slot], acc_ref, carry_ref)

    @pl.when(c == n_chunks - 1)
    def _():
        for p in range(H_GLA // 2):
            st_ref[seq, p] = stt_ref[seq, p].T

    @pl.when(j == dec_steps - 1)
    def _():
        own = _own_lanes(acc_ref.shape[1])
        osb_ref[...] = jnp.sum(jnp.where(own, acc_ref[...], 0.0), axis=0,
                               keepdims=True).astype(osb_ref.dtype)


def _gla_decode(qg, kg, la, vg, rg, gn, tril,
                q, k_new, v_new, cache_kt, cache_vt, pool_offset, page_table, bias8, tri,
                pq, pk, pv, sb_bias, ptri):
    n_seq, s, _ = qg.shape
    cs = min(GLA_CHUNK, s)
    n_chunks = s // cs
    nb, n_pages = page_table.shape
    _, w, page = cache_kt.shape
    pages = DEC_PAGES_PER_STEP
    dec_steps = n_pages // pages
    past_len = n_pages * page
    n_steps = n_seq * n_chunks
    assert s % cs == 0 and n_pages % pages == 0 and page == LANES
    assert n_steps == nb * dec_steps, "both jobs must have the same number of grid steps"

    sb_pairs = pq.shape[2] // LANES
    assert n_steps % (n_seq * sb_pairs) == 0
    sb_every = n_steps // (n_seq * sb_pairs)

    chunk = lambda n: pl.BlockSpec((None, cs, n), lambda i, *_: (i % n_seq, i // n_seq, 0))
    const = lambda shape: pl.BlockSpec(shape, lambda i, *_: (0,) * len(shape))
    vec = pl.BlockSpec((None, 1, w), lambda i, *_: (i // dec_steps, 0, 0))
    unit = pl.BlockSpec((None, s, LANES), lambda i, *_: (i // sb_every // sb_pairs, 0,
                                                         (i // sb_every) % sb_pairs))

    st_shape = (n_seq, H_GLA // 2, 2 * DK_GLA, DV_GLA)
    og, st, osb, po = pl.pallas_call(
        functools.partial(_gla_decode_body, pool_offset=pool_offset, past_len=past_len,
                          n_seq=n_seq, n_chunks=n_chunks, dec_steps=dec_steps,
                          sb_every=sb_every, sb_pairs=sb_pairs),
        grid_spec=pltpu.PrefetchScalarGridSpec(
            num_scalar_prefetch=2,
            grid=(n_steps,),
            in_specs=[chunk(WK_GLA), chunk(WK_GLA), chunk(WK_GLA), chunk(WV_GLA), chunk(WV_GLA),
                      const(gn.shape), const(tril.shape),
                      vec, vec, vec, const(bias8.shape), const(tri.shape),
                      pl.BlockSpec(memory_space=pl.ANY), pl.BlockSpec(memory_space=pl.ANY),
                      unit, unit, unit, const(ptri.shape)],
            out_specs=[chunk(WV_GLA), const(st_shape), vec, unit],
            scratch_shapes=[pltpu.VMEM((n_seq, H_GLA // 2, DV_GLA, 2 * DK_GLA), F32),
                            pltpu.VMEM((H_SB, w), F32), pltpu.VMEM((H_SB, LANES), F32),
                            pltpu.VMEM((2, pages, w, page), F32),
                            pltpu.VMEM((2, pages, w, page), F32),
                            pltpu.SemaphoreType.DMA((2, 2))],
        ),
        out_shape=[jax.ShapeDtypeStruct((n_seq, s, WV_GLA), BF16),
                   jax.ShapeDtypeStruct(st_shape, F32),
                   jax.ShapeDtypeStruct((nb, 1, w), BF16),
                   jax.ShapeDtypeStruct(pq.shape, BF16)],
        compiler_params=_params("arbitrary"),
        name="gla_decode",
    )(page_table, sb_bias, qg, kg, la, vg, rg, gn, tril, q, k_new, v_new, bias8, tri,
      cache_kt, cache_vt, pq, pk, pv, ptri)
    return og, st.reshape(n_seq, H_GLA, DK_GLA, DV_GLA), osb, po


def _merge_body(x_ref, osb_ref, og_ref, gt_ref, wosb_ref, wogl_ref, wout_ref, o_ref):
    d = x_ref.shape[1]
    g = gt_ref[...].astype(F32)
    m = g[:, :d] * _dot(osb_ref[...], wosb_ref[...]) + g[:, d:] * _dot(og_ref[...], wogl_ref[...])
    o_ref[...] = x_ref[...] + _dot(m.astype(BF16), wout_ref[...])


def _merge(x, osb, og, gt, w, tm):
    t, d = x.shape
    row = lambda n: pl.BlockSpec((tm, n), lambda i: (i, 0))
    return pl.pallas_call(
        _merge_body,
        grid=(t // tm,),
        in_specs=[row(d), row(W_SB), row(WV_GLA), row(2 * d),
                  _full(w["w_o_sb"].shape), _full(w["w_o_gla"].shape), _full(w["w_out"].shape)],
        out_specs=row(d),
        out_shape=jax.ShapeDtypeStruct((t, d), F32),
        compiler_params=_params("parallel"),
        name="merge",
    )(x, osb, og, gt, w["w_o_sb"], w["w_o_gla"], w["w_out"])


def _gla_step_body(q_ref, k_ref, la_ref, v_ref, r_ref, gn_ref, s0_ref, o_ref, s1_ref):
    nb = q_ref.shape[0]
    rowid = lax.broadcasted_iota(jnp.int32, (8, WK_GLA), 0)
    own = rowid == lax.broadcasted_iota(jnp.int32, (8, WK_GLA), 1) // DK_GLA
    vrow = lax.broadcasted_iota(jnp.int32, (8, DV_GLA), 0)
    ones = jnp.ones((8, DV_GLA), BF16)
    gn = gn_ref[...]

    def rows_of_heads(x):
        out = jnp.broadcast_to(x[:, :DV_GLA], (8, DV_GLA))
        for h in range(1, H_GLA):
            out = jnp.where(vrow == h, jnp.broadcast_to(x[:, h * DV_GLA:(h + 1) * DV_GLA],
                                                        (8, DV_GLA)), out)
        return out

    for i in range(nb):
        q = q_ref[i:i + 1, :]
        k = k_ref[i:i + 1, :]
        decay = jnp.exp(la_ref[i:i + 1, :])
        s0 = s0_ref[i]
        v8 = rows_of_heads(v_ref[i:i + 1, :])
        r8 = rows_of_heads(r_ref[i:i + 1, :])
        spread = lambda x: jnp.where(own, jnp.broadcast_to(x, (8, WK_GLA)), 0.0)
        qk = jnp.sum(spread(q * k), axis=1, keepdims=True)
        o = qk * v8 + _dot(spread(q * decay).astype(BF16), s0.astype(BF16))
        o = _rms(o, gn) * (r8 * jax.nn.sigmoid(r8))
        for h in range(H_GLA):
            o_ref[i:i + 1, h * DV_GLA:(h + 1) * DV_GLA] = o[h:h + 1, :].astype(o_ref.dtype)
        d_hi = decay.astype(BF16).astype(F32)
        dd = jnp.where(rowid == 0, jnp.broadcast_to(d_hi, (8, WK_GLA)),
                       jnp.where(rowid == 1, jnp.broadcast_to(decay - d_hi, (8, WK_GLA)), 0.0))
        decay_col = _dot_tn(dd.astype(BF16), ones)
        s1_ref[i] = decay_col * s0 + _dot_tn(spread(k).astype(BF16), v8.astype(BF16))


def _gla_step(qg, kg, la, vg, rg, gn, s0, layer, nb_blk=8):
    nb = qg.shape[0]
    blk0 = layer * (nb // nb_blk)
    row = lambda n: pl.BlockSpec((nb_blk, n), lambda i: (i, 0))
    st = pl.BlockSpec((nb_blk, WK_GLA, DV_GLA), lambda i: (i, 0, 0))
    st_in = pl.BlockSpec((nb_blk, WK_GLA, DV_GLA), lambda i: (blk0 + i, 0, 0))
    o, s1 = pl.pallas_call(
        _gla_step_body,
        grid=(nb // nb_blk,),
        in_specs=[row(WK_GLA), row(WK_GLA), row(WK_GLA), row(WV_GLA), row(WV_GLA),
                  _full(gn.shape), st_in],
        out_specs=[row(WV_GLA), st],
        out_shape=[jax.ShapeDtypeStruct((nb, WV_GLA), BF16),
                   jax.ShapeDtypeStruct((nb, WK_GLA, DV_GLA), F32)],
        compiler_params=_params("parallel"),
        name="gla_step",
    )(qg, kg, la, vg, rg, gn, s0)
    return o, s1.reshape(nb, H_GLA, DK_GLA, DV_GLA)


def _prep_layer(l, ffn1_norm, ffn1_w_in, ffn1_w_out, mix_norm, w_in, b_gate, q_norm, k_norm,
                sb_bias, w_a2, b_a, gla_norm, w_o_sb, w_o_gla, w_out, ffn2_norm, ffn2_w_in,
                ffn2_w_out):
    d = w_in.shape[1]
    d_ff = ffn1_w_out.shape[1]
    bf = lambda a: a.astype(BF16)
    o_gl = 3 * W_SB
    o_alr = o_gl + 2 * WK_GLA + 2 * WV_GLA
    o_gt = o_alr + GATE_RANK
    win = w_in[l]
    head = jnp.arange(W_SB) // DH_SB
    return {
        "ffn1": (ffn1_norm[l][None], bf(ffn1_w_in[l][:, :d_ff]), bf(ffn1_w_in[l][:, d_ff:]),
                 bf(ffn1_w_out[l])),
        "ffn2": (ffn2_norm[l][None], bf(ffn2_w_in[l][:, :d_ff]), bf(ffn2_w_in[l][:, d_ff:]),
                 bf(ffn2_w_out[l])),
        "mix_norm": mix_norm[l][None],
        "w_sb": bf(win[:, :o_gl]),
        "w_gl": bf(win[:, o_gl:o_alr]),
        "w_alr": bf(jnp.pad(win[:, o_alr:o_gt], ((0, 0), (0, LANES - GATE_RANK)))),
        "w_a2": bf(jnp.pad(w_a2[l], ((0, LANES - GATE_RANK), (0, 0)))),
        "b_a": b_a[l][None],
        "w_gt": bf(win[:, o_gt:]),
        "b_gt": b_gate[l][None],
        "q_norm": jnp.tile(q_norm[l], H_SB)[None],
        "k_norm": jnp.tile(k_norm[l], H_SB)[None],
        "blk": (head[:, None] == head[None, :]).astype(BF16),
        "sb_bias": sb_bias[l],
        "gla_norm": gla_norm[l][None],
        "w_o_sb": bf(w_o_sb[l]),
        "w_o_gla": bf(w_o_gla[l]),
        "w_out": bf(w_out[l]),
    }


def _strict_upper(n):
    i = jnp.arange(n)
    return (i[:, None] > i[None, :]).astype(BF16)


def _layer(xp, xs, cache_k, cache_v, state, page_table, layer, w):
    b, s, d = xp.shape
    nb, s1, _ = xs.shape
    assert s1 == 1
    tm = 512
    depth, n_pool, page = cache_k.shape[:3]

    x2 = _ffn(xp.reshape(b * s, d), *w["ffn1"], 2 * tm)
    q, kt, vt, kb, vb, qg, kg, la, vg, rg, gt = _inproj(x2, w, tm, seq=s)
    y2 = _ffn(xs.reshape(nb, d), *w["ffn1"], nb)
    qd, kd, vd, _, _, qgd, kgd, lad, vgd, rgd, gtd = _inproj(y2, w, nb)

    sh = lambda a: a.reshape(b, s, a.shape[-1])

    cs = min(GLA_CHUNK, s)
    i = jnp.arange(cs)
    tril = (i[None, :] <= i[:, None]).astype(BF16)
    bias8 = jnp.broadcast_to(w["sb_bias"][:, None], (H_SB, LANES))
    pool_t = lambda c: jnp.transpose(c, (0, 1, 3, 4, 2)).reshape(depth * n_pool, W_SB, page)
    vec = lambda a: a.reshape(nb, 1, W_SB)
    og, stp, osd, osb = _gla_decode(sh(qg), sh(kg), sh(la), sh(vg), sh(rg), w["gla_norm"], tril,
                                    vec(qd), vec(kd), vec(vd), pool_t(cache_k), pool_t(cache_v),
                                    layer * n_pool, page_table, bias8, _strict_upper(page),
                                    sh(q), sh(kb), sh(vb), w["sb_bias"], _strict_upper(SB_TILE))
    ogd, std = _gla_step(qgd, kgd, lad, vgd, rgd, w["gla_norm"],
                         state.reshape(depth * nb, WK_GLA, DV_GLA), layer)

    x2 = _merge(x2, osb.reshape(b * s, W_SB), og.reshape(b * s, WV_GLA), gt, w, 2 * tm)
    x2 = _ffn(x2, *w["ffn2"], 2 * tm)
    y2 = _merge(y2, osd.reshape(nb, W_SB), ogd, gtd, w, nb)
    y2 = _ffn(y2, *w["ffn2"], nb)

    heads = lambda a: jnp.transpose(a.reshape(b, H_SB, DH_SB, s), (0, 3, 1, 2))
    new = lambda a: a.reshape(nb, 1, H_SB, DH_SB)
    return (x2.reshape(b, s, d), y2.reshape(nb, 1, d),
            (heads(kt), heads(vt), stp, new(kd), new(vd), std))


def kernel(x_prompt, x_sample, cache_k, cache_v, state_gla, page_table, ffn1_norm, ffn1_w_in,
           ffn1_w_out, mix_norm, w_in, b_gate, q_norm, k_norm, sb_bias, w_a2, b_a, gla_norm,
           w_o_sb, w_o_gla, w_out, ffn2_norm, ffn2_w_in, ffn2_w_out):
    depth = w_in.shape[0]
    yp, ys = x_prompt, x_sample
    outs = [[] for _ in range(6)]
    for l in range(depth):
        w = _prep_layer(l, ffn1_norm, ffn1_w_in, ffn1_w_out, mix_norm, w_in, b_gate, q_norm,
                        k_norm, sb_bias, w_a2, b_a, gla_norm, w_o_sb, w_o_gla, w_out, ffn2_norm,
                        ffn2_w_in, ffn2_w_out)
        yp, ys, vals = _layer(yp, ys, cache_k, cache_v, state_gla, page_table, l, w)
        for acc, val in zip(outs, vals):
            acc.append(val)
    return (yp, ys) + tuple(jnp.stack(o) for o in outs)
```
